```python
import jax, jax.numpy as jnp
from jax import lax
import numpy as np

D_MODEL = 2048
BATCH = 1
SEQ = 8192
DEPTH = 4
DEC_BATCH = 32
DEC_SEQ = 64
PAST_LEN = 2048

CHUNK = 64
N_MIXERS = 2
N_SB_LAYERS = (DEPTH + 1) // 2
N_CM_LAYERS = DEPTH // 2
N_HEADS = 16
HEAD_DIM = D_MODEL // N_HEADS
Q_BLOCK = 128
CM_CHUNK = 128
CM_WIDTH = 2 * D_MODEL
CM_GROUPS = 16
CM_GROUP_DIM = CM_WIDTH // CM_GROUPS
N_EXPERTS = 64
N_EXPERT_GROUPS = 8
TOPK_GROUPS = 4
TOP_K = 8
D_EXPERT = 512
D_SHARED = 512
ROUTED_SCALE = 2.5
ALPHA = (2 * DEPTH) ** 0.25
BETA = (8 * DEPTH) ** -0.25
LN_EPS = 1e-5

kernel_name = "stickbreak_chunkmlp_moe_stream_step"


def layer_norm(x, g, b):
    xf = x.astype(jnp.float32)
    mu = jnp.mean(xf, axis=-1, keepdims=True)
    var = jnp.mean(jnp.square(xf - mu), axis=-1, keepdims=True)
    return ((xf - mu) * lax.rsqrt(var + LN_EPS)).astype(x.dtype) * g + b


def adaln(c, w, b):
    mod = jax.nn.silu(c) @ w + b
    shift, scale, gate = jnp.split(mod[:, None, :], 3, axis=-1)
    return shift, 1.0 + scale, 1.0 + gate


def stick_breaking_block(q, k, v, q_pos, k_pos):
    z = jnp.einsum('bqhd,bkhd->bhqk', q, k).astype(jnp.float32) * (HEAD_DIM ** -0.5)
    mask = k_pos[None, :] < q_pos[:, None]
    log_beta = jax.nn.log_sigmoid(z)
    log_1mb = jnp.where(mask, jax.nn.log_sigmoid(-z), 0.0)
    log_stick = lax.cumsum(log_1mb, axis=3, reverse=True) - log_1mb
    a = jnp.where(mask, jnp.exp(log_beta + log_stick), 0.0)
    return jnp.einsum('bhqk,bkhd->bqhd', a.astype(v.dtype), v)


def sb_project(h, w_qkv):
    B, S, _ = h.shape
    qkv = (h @ w_qkv).reshape(B, S, 3, N_HEADS, HEAD_DIM)
    return qkv[:, :, 0], qkv[:, :, 1], qkv[:, :, 2]


def sb_prompt(h, w_qkv, w_o):
    B, S, _ = h.shape
    q, k, v = sb_project(h, w_qkv)
    nb = S // Q_BLOCK
    q_blocks = q.reshape(B, nb, Q_BLOCK, N_HEADS, HEAD_DIM).swapaxes(0, 1)
    pos = jnp.arange(S)
    o = lax.map(lambda a: stick_breaking_block(a[0], k, v, a[1], pos),
                (q_blocks, pos.reshape(nb, Q_BLOCK)))
    o = o.swapaxes(0, 1).reshape(B, S, D_MODEL)
    return o @ w_o, k, v


def sb_sample(h, cache_k, cache_v, w_qkv, w_o):
    B, T, _ = h.shape
    P = cache_k.shape[1]
    q, k, v = sb_project(h, w_qkv)
    q_pos = P + jnp.arange(T)
    k_pos = jnp.arange(P + T)

    def one_stream(args):
        qb, ckb, cvb, kb, vb = args
        k_all = jnp.concatenate([ckb, kb], axis=0)[None]
        v_all = jnp.concatenate([cvb, vb], axis=0)[None]
        return stick_breaking_block(qb[None], k_all, v_all, q_pos, k_pos)[0]

    o = lax.map(one_stream, (q, cache_k, cache_v, k, v))
    return o.reshape(B, T, D_MODEL) @ w_o, k, v


def chunk_mlp(h, w_uv, b_uv, ln_v_g, ln_v_b, w_s, b_s, w_out, chunk_len):
    B, S, _ = h.shape
    zz = jax.nn.gelu(h @ w_uv + b_uv)
    u, v = jnp.split(zz, 2, axis=-1)
    v = layer_norm(v, ln_v_g, ln_v_b)
    n = S // chunk_len
    vc = v.reshape(B, n, chunk_len, CM_GROUPS, CM_GROUP_DIM)
    pos = jnp.arange(chunk_len)
    mask = (pos[None, :] // CHUNK) <= (pos[:, None] // CHUNK)
    ws = jnp.where(mask, w_s[:, :chunk_len, :chunk_len], 0.0)
    bias = b_s[:, :chunk_len].T[None, None, :, :, None]
    mixed = jnp.einsum('gts,bnsgc->bntgc', ws, vc) + bias
    gated = u * mixed.reshape(B, S, CM_WIDTH)
    return gated @ w_out, v


def moe(h, w_router, b_router, w_gate, w_up, w_down, ws_gate, ws_up, ws_down):
    B, S, D = h.shape
    x = h.reshape(B * S, D)
    T = x.shape[0]
    scores = jax.nn.sigmoid((x @ w_router).astype(jnp.float32))
    sel = scores + b_router.astype(jnp.float32)
    grp = sel.reshape(T, N_EXPERT_GROUPS, N_EXPERTS // N_EXPERT_GROUPS)
    grp_score = jnp.sum(lax.top_k(grp, 2)[0], axis=-1)
    _, grp_idx = lax.top_k(grp_score, TOPK_GROUPS)
    grp_mask = jnp.sum(jax.nn.one_hot(grp_idx, N_EXPERT_GROUPS, dtype=jnp.float32), axis=-2) > 0
    expert_mask = jnp.repeat(grp_mask, N_EXPERTS // N_EXPERT_GROUPS, axis=-1)
    _, top_idx = lax.top_k(jnp.where(expert_mask, sel, -jnp.inf), TOP_K)
    top_w = jnp.take_along_axis(scores, top_idx, axis=-1)
    top_w = top_w / jnp.sum(top_w, axis=-1, keepdims=True) * ROUTED_SCALE
    combine = jnp.sum(jax.nn.one_hot(top_idx, N_EXPERTS, dtype=jnp.float32) * top_w[..., None], axis=-2)

    def add_expert(acc, p):
        wg, wu, wd, ce = p
        hid = jax.nn.silu(x @ wg) * (x @ wu)
        return acc + (hid @ wd) * ce[:, None].astype(x.dtype), None

    y, _ = lax.scan(add_expert, jnp.zeros_like(x), (w_gate, w_up, w_down, combine.T))
    shared = (jax.nn.silu(x @ ws_gate) * (x @ ws_up)) @ ws_down
    return (y + shared).reshape(B, S, D)


def setup_inputs(seed: int = 0) -> dict:
    key = jax.random.key(seed)
    ks = jax.random.split(key, 32)
    D = D_MODEL

    def nrm(k, shape, s):
        return jax.random.normal(k, shape, jnp.float32) * s

    return {
        "x_prompt": nrm(ks[0], (BATCH, SEQ, D), 1.0),
        "x_sample": nrm(ks[1], (DEC_BATCH, DEC_SEQ, D), 1.0),
        "c_prompt": nrm(ks[2], (BATCH, D), 1.0),
        "c_sample": nrm(ks[3], (DEC_BATCH, D), 1.0),
        "cache_k": nrm(ks[4], (N_SB_LAYERS, DEC_BATCH, PAST_LEN, N_HEADS, HEAD_DIM), 1.0),
        "cache_v": nrm(ks[5], (N_SB_LAYERS, DEC_BATCH, PAST_LEN, N_HEADS, HEAD_DIM), 1.0),
        "w_ada_mix": nrm(ks[6], (DEPTH, D, 3 * D), 0.1 * D ** -0.5),
        "b_ada_mix": nrm(ks[7], (DEPTH, 3 * D), 0.01),
        "w_ada_ffn": nrm(ks[8], (DEPTH, D, 3 * D), 0.1 * D ** -0.5),
        "b_ada_ffn": nrm(ks[9], (DEPTH, 3 * D), 0.01),
        "ln_mix_g": 1.0 + nrm(ks[10], (DEPTH, D), 0.01),
        "ln_mix_b": nrm(ks[11], (DEPTH, D), 0.01),
        "ln_ffn_g": 1.0 + nrm(ks[12], (DEPTH, D), 0.01),
        "ln_ffn_b": nrm(ks[13], (DEPTH, D), 0.01),
        "w_qkv": nrm(ks[14], (N_SB_LAYERS, D, 3 * D), D ** -0.5),
        "w_o": nrm(ks[15], (N_SB_LAYERS, D, D), BETA * D ** -0.5),
        "w_uv": nrm(ks[16], (N_CM_LAYERS, D, 2 * CM_WIDTH), D ** -0.5),
        "b_uv": nrm(ks[17], (N_CM_LAYERS, 2 * CM_WIDTH), 0.01),
        "ln_v_g": 1.0 + nrm(ks[18], (N_CM_LAYERS, CM_WIDTH), 0.01),
        "ln_v_b": nrm(ks[19], (N_CM_LAYERS, CM_WIDTH), 0.01),
        "w_s": nrm(ks[20], (N_CM_LAYERS, CM_GROUPS, CM_CHUNK, CM_CHUNK), CM_CHUNK ** -0.5),
        "b_s": 1.0 + nrm(ks[21], (N_CM_LAYERS, CM_GROUPS, CM_CHUNK), 0.01),
        "w_cm_out": nrm(ks[22], (N_CM_LAYERS, CM_WIDTH, D), BETA * CM_WIDTH ** -0.5),
        "w_router": nrm(ks[23], (DEPTH, D, N_EXPERTS), D ** -0.5),
        "b_router": nrm(ks[24], (DEPTH, N_EXPERTS), 0.01),
        "w_gate": nrm(ks[25], (DEPTH, N_EXPERTS, D, D_EXPERT), D ** -0.5),
        "w_up": nrm(ks[26], (DEPTH, N_EXPERTS, D, D_EXPERT), D ** -0.5),
        "w_down": nrm(ks[27], (DEPTH, N_EXPERTS, D_EXPERT, D), BETA * D_EXPERT ** -0.5),
        "ws_gate": nrm(ks[28], (DEPTH, D, D_SHARED), D ** -0.5),
        "ws_up": nrm(ks[29], (DEPTH, D, D_SHARED), D ** -0.5),
        "ws_down": nrm(ks[30], (DEPTH, D_SHARED, D), BETA * D_SHARED ** -0.5),
    }


def reference(x_prompt, x_sample, c_prompt, c_sample, cache_k, cache_v,
              w_ada_mix, b_ada_mix, w_ada_ffn, b_ada_ffn,
              ln_mix_g, ln_mix_b, ln_ffn_g, ln_ffn_b,
              w_qkv, w_o,
              w_uv, b_uv, ln_v_g, ln_v_b, w_s, b_s, w_cm_out,
              w_router, b_router, w_gate, w_up, w_down, ws_gate, ws_up, ws_down):
    xp, xs = x_prompt, x_sample
    T_new = x_sample.shape[1]
    kp_list, vp_list, ks_list, vs_list, cmp_list, cms_list = [], [], [], [], [], []
    for i in range(DEPTH):
        j = i // N_MIXERS
        sh_p, sc_p, g_p = adaln(c_prompt, w_ada_mix[i], b_ada_mix[i])
        sh_s, sc_s, g_s = adaln(c_sample, w_ada_mix[i], b_ada_mix[i])
        hp = xp * sc_p + sh_p
        hs = xs * sc_s + sh_s
        if i % N_MIXERS == 0:
            op, kp, vp = sb_prompt(hp, w_qkv[j], w_o[j])
            os_, ks_, vs_ = sb_sample(hs, cache_k[j], cache_v[j], w_qkv[j], w_o[j])
            kp_list.append(kp)
            vp_list.append(vp)
            ks_list.append(ks_)
            vs_list.append(vs_)
        else:
            op, cvp = chunk_mlp(hp, w_uv[j], b_uv[j], ln_v_g[j], ln_v_b[j], w_s[j], b_s[j], w_cm_out[j], CM_CHUNK)
            os_, cvs = chunk_mlp(hs, w_uv[j], b_uv[j], ln_v_g[j], ln_v_b[j], w_s[j], b_s[j], w_cm_out[j], T_new)
            cmp_list.append(cvp[:, -CM_CHUNK:])
            cms_list.append(cvs)
        xp = layer_norm(ALPHA * xp + g_p * op, ln_mix_g[i], ln_mix_b[i])
        xs = layer_norm(ALPHA * xs + g_s * os_, ln_mix_g[i], ln_mix_b[i])
        sh_p, sc_p, g_p = adaln(c_prompt, w_ada_ffn[i], b_ada_ffn[i])
        sh_s, sc_s, g_s = adaln(c_sample, w_ada_ffn[i], b_ada_ffn[i])
        fp = moe(xp * sc_p + sh_p, w_router[i], b_router[i], w_gate[i], w_up[i], w_down[i], ws_gate[i], ws_up[i], ws_down[i])
        fs = moe(xs * sc_s + sh_s, w_router[i], b_router[i], w_gate[i], w_up[i], w_down[i], ws_gate[i], ws_up[i], ws_down[i])
        xp = layer_norm(ALPHA * xp + g_p * fp, ln_ffn_g[i], ln_ffn_b[i])
        xs = layer_norm(ALPHA * xs + g_s * fs, ln_ffn_g[i], ln_ffn_b[i])
    y_prompt, y_sample = xp, xs
    new_k_prompt = jnp.stack(kp_list)
    new_v_prompt = jnp.stack(vp_list)
    new_k_sample = jnp.stack(ks_list)
    new_v_sample = jnp.stack(vs_list)
    new_cmv_prompt = jnp.stack(cmp_list)
    new_cmv_sample = jnp.stack(cms_list)
    return (y_prompt, y_sample, new_k_prompt, new_v_prompt, new_k_sample, new_v_sample, new_cmv_prompt, new_cmv_sample)
```

```python
import functools

import jax
import jax.numpy as jnp
from jax import lax
from jax.experimental import pallas as pl
from jax.experimental.pallas import tpu as pltpu

F32 = jnp.float32
BF16 = jnp.bfloat16
I32 = jnp.int32

DEPTH = 4
N_HEADS = 16
HEAD_DIM = 128
CHUNK = 64
CM_CHUNK = 128
CM_GROUPS = 16
N_EXPERTS = 64
N_EXPERT_GROUPS = 8
TOPK_GROUPS = 4
TOP_K = 8
ROUTED_SCALE = 2.5
ALPHA = (2 * DEPTH) ** 0.25
LN_EPS = 1e-5

GROUP = 64
MIB = 1024 * 1024
STICK_DONE = -104.0
ATT_BLOCK = 256
EXPERT_TILE = 256


def _cp(vmem_mib, sem):
    return pltpu.CompilerParams(dimension_semantics=sem, vmem_limit_bytes=vmem_mib * MIB)


def _ln(y, g, b):
    mu = jnp.mean(y, axis=-1, keepdims=True)
    yc = y - mu
    var = jnp.mean(yc * yc, axis=-1, keepdims=True)
    return yc * lax.rsqrt(var + LN_EPS) * g + b


def _modulate(x, sh_ref, sc_ref):
    tm, d = x.shape
    g = sh_ref.shape[1]
    xg = x.reshape(g, tm // g, d)
    return (xg * sc_ref[0] + sh_ref[0]).reshape(tm, d)


def _residual_ln(x, f, gate_ref, g_ref, b_ref):
    tm, d = x.shape
    g = gate_ref.shape[1]
    y = ALPHA * x + (gate_ref[0] * f.reshape(g, tm // g, d)).reshape(tm, d)
    return _ln(y, g_ref[...], b_ref[...])


def _ada_kernel(c_ref, w_ref, b_ref, o_ref, *, d_model, tn):
    j = pl.program_id(1)
    a = jax.nn.silu(c_ref[...]).astype(BF16)
    acc = jnp.dot(a, w_ref[0].astype(BF16), preferred_element_type=F32) + b_ref[0]
    o_ref[0] = acc + jnp.where(j * tn >= d_model, 1.0, 0.0).astype(F32)


def _ada(c_all, w, b):
    depth, d, n = w.shape
    r = c_all.shape[0]
    tn = 1024
    return pl.pallas_call(
        functools.partial(_ada_kernel, d_model=d, tn=tn),
        out_shape=jax.ShapeDtypeStruct((depth, r, n), F32),
        grid=(depth, n // tn),
        in_specs=[
            pl.BlockSpec((r, d), lambda l, j: (0, 0)),
            pl.BlockSpec((1, d, tn), lambda l, j: (l, 0, j)),
            pl.BlockSpec((1, 1, tn), lambda l, j: (l, 0, j)),
        ],
        out_specs=pl.BlockSpec((1, r, tn), lambda l, j: (l, 0, j)),
        compiler_params=_cp(40, ("parallel", "parallel")),
        name="ada",
    )(c_all, w, b.reshape(depth, 1, n))


def _modmm_kernel(x_ref, sh_ref, sc_ref, w_ref, b_ref, o_ref, *, gelu):
    h = _modulate(x_ref[...], sh_ref, sc_ref).astype(BF16)
    acc = jnp.dot(h, w_ref[...], preferred_element_type=F32) + b_ref[...]
    if gelu:
        acc = jax.nn.gelu(acc)
    o_ref[...] = acc.astype(o_ref.dtype)


def _modmm(x, mods, layer, w, b, out_dtype, gelu):
    t, d = x.shape
    n = w.shape[1]
    tm, tn = 512, 2048
    g = tm // GROUP
    return pl.pallas_call(
        functools.partial(_modmm_kernel, gelu=gelu),
        out_shape=jax.ShapeDtypeStruct((t, n), out_dtype),
        grid=(n // tn, t // tm),
        in_specs=[
            pl.BlockSpec((tm, d), lambda j, i: (i, 0)),
            pl.BlockSpec((1, g, 1, d), lambda j, i: (layer, i, 0, 0)),
            pl.BlockSpec((1, g, 1, d), lambda j, i: (layer, i, 0, 1)),
            pl.BlockSpec((d, tn), lambda j, i: (0, j)),
            pl.BlockSpec((1, tn), lambda j, i: (0, j)),
        ],
        out_specs=pl.BlockSpec((tm, tn), lambda j, i: (i, j)),
        compiler_params=_cp(48, ("parallel", "parallel")),
        name="modmm",
    )(x, mods, mods, w, b)


def _sb_block(q, k, v, csum, u, mask):
    z = lax.dot_general(q, k, (((1,), (1,)), ((), ())), preferred_element_type=F32)
    z = z * (HEAD_DIM ** -0.5)
    t = jnp.log1p(jnp.exp(-jnp.abs(z)))
    log_beta = -(jnp.maximum(-z, 0.0) + t)
    l1 = -(jnp.maximum(z, 0.0) + t)
    if mask is not None:
        l1 = jnp.where(mask, l1, 0.0)
    hi = l1.astype(BF16)
    lo = (l1 - hi.astype(F32)).astype(BF16)
    s = jnp.dot(hi, u, preferred_element_type=F32) + jnp.dot(lo, u, preferred_element_type=F32)
    a = jnp.exp(log_beta + s + csum)
    if mask is not None:
        a = jnp.where(mask, a, 0.0)
    o = jnp.dot(a.astype(BF16), v, preferred_element_type=F32)
    return o, csum + s[:, :1] + l1[:, :1]


def _causal_mask(n):
    row = lax.broadcasted_iota(I32, (n, n), 0)
    col = lax.broadcasted_iota(I32, (n, n), 1)
    return col < row


def _attn_prompt_kernel(q_ref, k_ref, v_ref, u_ref, o_ref):
    i = pl.program_id(1)
    bq = q_ref.shape[0]
    q = q_ref[...].astype(BF16)
    u = u_ref[...]

    def kv(b):
        start = pl.multiple_of(b * bq, bq)
        return (k_ref[pl.ds(start, bq), :].astype(BF16), v_ref[pl.ds(start, bq), :].astype(BF16))

    k0, v0 = kv(i)
    o, cs = _sb_block(q, k0, v0, jnp.zeros((bq, 1), F32), u, _causal_mask(bq))

    def cond(c):
        b, _, cs = c
        return jnp.logical_and(b >= 0, jnp.max(cs) > STICK_DONE)

    def body(c):
        b, o, cs = c
        kb, vb = kv(b)
        do, cs = _sb_block(q, kb, vb, cs, u, None)
        return b - 1, o + do, cs

    _, o, _ = lax.while_loop(cond, body, (i - 1, o, cs))
    o_ref[...] = o.astype(o_ref.dtype)


def _strict_lower(n):
    r = jnp.arange(n)
    return (r[:, None] > r[None, :]).astype(BF16)


def _attn_prompt(qkv, n_prompt):
    d = N_HEADS * HEAD_DIM
    bq = ATT_BLOCK
    return pl.pallas_call(
        _attn_prompt_kernel,
        out_shape=jax.ShapeDtypeStruct((n_prompt, d), BF16),
        grid=(N_HEADS, n_prompt // bq),
        in_specs=[
            pl.BlockSpec((bq, HEAD_DIM), lambda h, i: (i, h)),
            pl.BlockSpec((n_prompt, HEAD_DIM), lambda h, i: (0, N_HEADS + h)),
            pl.BlockSpec((n_prompt, HEAD_DIM), lambda h, i: (0, 2 * N_HEADS + h)),
            pl.BlockSpec((bq, bq), lambda h, i: (0, 0)),
        ],
        out_specs=pl.BlockSpec((bq, HEAD_DIM), lambda h, i: (i, h)),
        compiler_params=_cp(40, ("parallel", "parallel")),
        name="attn_prompt",
    )(qkv, qkv, qkv, _strict_lower(bq))


def _attn_sample_kernel(q_ref, kn_ref, vn_ref, ck_ref, cv_ref, un_ref, uc_ref, o_ref, cs_ref):
    h = pl.program_id(1)
    tq = q_ref.shape[0]
    bk = uc_ref.shape[0]
    q = q_ref[...].astype(BF16)
    o, cs = _sb_block(q, kn_ref[...].astype(BF16), vn_ref[...].astype(BF16),
                      jnp.zeros((tq, 1), F32), un_ref[...], _causal_mask(tq))
    kc = ck_ref[0, 0, pl.ds(h, bk, stride=N_HEADS), :].astype(BF16)
    vc = cv_ref[0, 0, pl.ds(h, bk, stride=N_HEADS), :].astype(BF16)
    do, cs = _sb_block(q, kc, vc, cs, uc_ref[...], None)
    o_ref[...] = o + do
    cs_ref[...] = jnp.broadcast_to(cs, cs_ref.shape)


def _attn_sample(qkv, cache_k, cache_v, layer, n_prompt, n_streams, t_new):
    d = N_HEADS * HEAD_DIM
    past = cache_k.shape[2] // N_HEADS
    bk = ATT_BLOCK
    pb = n_prompt // t_new
    last = past // bk - 1
    return pl.pallas_call(
        _attn_sample_kernel,
        out_shape=(jax.ShapeDtypeStruct((n_streams * t_new, d), F32),
                   jax.ShapeDtypeStruct((n_streams * t_new, d), F32)),
        grid=(n_streams, N_HEADS),
        in_specs=[
            pl.BlockSpec((t_new, HEAD_DIM), lambda s, h: (pb + s, h)),
            pl.BlockSpec((t_new, HEAD_DIM), lambda s, h: (pb + s, N_HEADS + h)),
            pl.BlockSpec((t_new, HEAD_DIM), lambda s, h: (pb + s, 2 * N_HEADS + h)),
            pl.BlockSpec((1, 1, bk * N_HEADS, HEAD_DIM), lambda s, h: (layer, s, last, 0)),
            pl.BlockSpec((1, 1, bk * N_HEADS, HEAD_DIM), lambda s, h: (layer, s, last, 0)),
            pl.BlockSpec((t_new, t_new), lambda s, h: (0, 0)),
            pl.BlockSpec((bk, bk), lambda s, h: (0, 0)),
        ],
        out_specs=(pl.BlockSpec((t_new, HEAD_DIM), lambda s, h: (s, h)),
                   pl.BlockSpec((t_new, HEAD_DIM), lambda s, h: (s, h))),
        compiler_params=_cp(32, ("parallel", "arbitrary")),
        name="attn_sample",
    )(qkv, qkv, qkv, cache_k, cache_v, _strict_lower(t_new), _strict_lower(bk))


def _attn_sample_rest_kernel(q_ref, oin_ref, csin_ref, ck_ref, cv_ref, u_ref, o_ref, cs_ref):
    b = pl.program_id(1)
    bk = u_ref.shape[0]

    @pl.when(b == 0)
    def _():
        o_ref[...] = oin_ref[...]
        cs_ref[...] = csin_ref[...]

    for h in range(N_HEADS):
        cols = slice(h * HEAD_DIM, (h + 1) * HEAD_DIM)
        cs = cs_ref[:, cols][:, :1]

        @pl.when(jnp.max(cs) > STICK_DONE)
        def _(h=h, cols=cols, cs=cs):
            q = q_ref[:, cols].astype(BF16)
            kc = ck_ref[0, 0, pl.ds(h, bk, stride=N_HEADS), :].astype(BF16)
            vc = cv_ref[0, 0, pl.ds(h, bk, stride=N_HEADS), :].astype(BF16)
            do, cs2 = _sb_block(q, kc, vc, cs, u_ref[...], None)
            o_ref[:, cols] += do
            cs_ref[:, cols] = jnp.broadcast_to(cs2, (cs2.shape[0], HEAD_DIM))


def _attn_sample_rest(qkv, o_part, cs_part, cache_k, cache_v, layer, n_prompt, n_streams, t_new):
    d = N_HEADS * HEAD_DIM
    past = cache_k.shape[2] // N_HEADS
    bk = ATT_BLOCK
    pb = n_prompt // t_new
    nb = past // bk - 1
    o, _ = pl.pallas_call(
        _attn_sample_rest_kernel,
        out_shape=(jax.ShapeDtypeStruct((n_streams * t_new, d), F32),
                   jax.ShapeDtypeStruct((n_streams * t_new, d), F32)),
        grid=(n_streams, nb),
        in_specs=[
            pl.BlockSpec((t_new, d), lambda s, b: (pb + s, 0)),
            pl.BlockSpec((t_new, d), lambda s, b: (s, 0)),
            pl.BlockSpec((t_new, d), lambda s, b: (s, 0)),
            pl.BlockSpec((1, 1, bk * N_HEADS, HEAD_DIM), lambda s, b: (layer, s, nb - 1 - b, 0)),
            pl.BlockSpec((1, 1, bk * N_HEADS, HEAD_DIM), lambda s, b: (layer, s, nb - 1 - b, 0)),
            pl.BlockSpec((bk, bk), lambda s, b: (0, 0)),
        ],
        out_specs=(pl.BlockSpec((t_new, d), lambda s, b: (s, 0)),
                   pl.BlockSpec((t_new, d), lambda s, b: (s, 0))),
        compiler_params=_cp(32, ("parallel", "arbitrary")),
        name="attn_sample_rest",
    )(qkv, o_part, cs_part, cache_k, cache_v, _strict_lower(bk))
    return o


def _proj_ln_kernel(a_ref, x_ref, gate_ref, w_ref, g_ref, b_ref, o_ref):
    f = jnp.dot(a_ref[...].astype(BF16), w_ref[...], preferred_element_type=F32)
    o_ref[...] = _residual_ln(x_ref[...], f, gate_ref, g_ref, b_ref)


def _proj_ln(a, x, mods, layer, w, g, b):
    t, d = x.shape
    kdim = a.shape[1]
    tm = 256
    ng = tm // GROUP
    return pl.pallas_call(
        _proj_ln_kernel,
        out_shape=jax.ShapeDtypeStruct((t, d), F32),
        grid=(t // tm,),
        in_specs=[
            pl.BlockSpec((tm, kdim), lambda i: (i, 0)),
            pl.BlockSpec((tm, d), lambda i: (i, 0)),
            pl.BlockSpec((1, ng, 1, d), lambda i: (layer, i, 0, 2)),
            pl.BlockSpec((kdim, d), lambda i: (0, 0)),
            pl.BlockSpec((1, d), lambda i: (0, 0)),
            pl.BlockSpec((1, d), lambda i: (0, 0)),
        ],
        out_specs=pl.BlockSpec((tm, d), lambda i: (i, 0)),
        compiler_params=_cp(40, ("parallel",)),
        name="proj_ln",
    )(a, x, mods, w, g, b)


def _cm_mix_kernel(u_ref, vraw_ref, x_ref, gate_ref, mx_ref, bias_ref, lvg_ref, lvb_ref,
                   wo_ref, g_ref, b_ref, o_ref, v_ref, gated_ref):
    tm, width = u_ref.shape
    gw = width // CM_GROUPS
    v = _ln(vraw_ref[...].astype(F32), lvg_ref[...], lvb_ref[...])
    v_ref[...] = v
    vb = v.astype(BF16)
    for c in range(tm // CM_CHUNK):
        rows = slice(c * CM_CHUNK, (c + 1) * CM_CHUNK)
        for g in range(CM_GROUPS):
            cols = slice(g * gw, (g + 1) * gw)
            mixed = jnp.dot(mx_ref[0, g], vb[rows, cols], preferred_element_type=F32)
            mixed = mixed + bias_ref[0, :, cols]
            gated_ref[rows, cols] = (u_ref[rows, cols].astype(F32) * mixed).astype(BF16)
    f = jnp.dot(gated_ref[...], wo_ref[...], preferred_element_type=F32)
    o_ref[...] = _residual_ln(x_ref[...], f, gate_ref, g_ref, b_ref)


def _cm_mix(zz, x, mods, layer, mx, bias, lvg, lvb, wo, g, b, n_prompt):
    t, d = x.shape
    width = zz.shape[1] // 2
    tm = CM_CHUNK
    ng = tm // GROUP
    n_prompt_tiles = n_prompt // tm
    kind = lambda i: jnp.where(i < n_prompt_tiles, 0, 1)
    return pl.pallas_call(
        _cm_mix_kernel,
        out_shape=(jax.ShapeDtypeStruct((t, d), F32), jax.ShapeDtypeStruct((t, width), F32)),
        grid=(t // tm,),
        in_specs=[
            pl.BlockSpec((tm, width), lambda i: (i, 0)),
            pl.BlockSpec((tm, width), lambda i: (i, 1)),
            pl.BlockSpec((tm, d), lambda i: (i, 0)),
            pl.BlockSpec((1, ng, 1, d), lambda i: (layer, i, 0, 2)),
            pl.BlockSpec((1, CM_GROUPS, CM_CHUNK, CM_CHUNK), lambda i: (kind(i), 0, 0, 0)),
            pl.BlockSpec((1, CM_CHUNK, width), lambda i: (kind(i), 0, 0)),
            pl.BlockSpec((1, width), lambda i: (0, 0)),
            pl.BlockSpec((1, width), lambda i: (0, 0)),
            pl.BlockSpec((width, d), lambda i: (0, 0), pipeline_mode=pl.Buffered(1)),
            pl.BlockSpec((1, d), lambda i: (0, 0)),
            pl.BlockSpec((1, d), lambda i: (0, 0)),
        ],
        out_specs=(pl.BlockSpec((tm, d), lambda i: (i, 0)),
                   pl.BlockSpec((tm, width), lambda i: (i, 0))),
        scratch_shapes=[pltpu.VMEM((tm, width), BF16)],
        compiler_params=_cp(48, ("parallel",)),
        name="cm_mix",
    )(zz, zz, x, mods, mx, bias, lvg, lvb, wo, g, b)


def _split_dot(x, w):
    xh = x.astype(BF16)
    xl = (x - xh.astype(F32)).astype(BF16)
    wh = w.astype(BF16)
    wl = (w - wh.astype(F32)).astype(BF16)
    d = lambda a, b: jnp.dot(a, b, preferred_element_type=F32)
    return d(xh, wh) + (d(xh, wl) + d(xl, wh))


def _first_argmax(m, lane, n):
    m1 = jnp.max(m, axis=1, keepdims=True)
    i1 = jnp.min(jnp.where(m == m1, lane, n), axis=1, keepdims=True)
    return m1, lane == i1


def _router_kernel(x_ref, sh_ref, sc_ref, wr_ref, br_ref, xm_ref, cw_ref, mk_ref, cnt_ref):
    i = pl.program_id(0)
    xm = _modulate(x_ref[...], sh_ref, sc_ref)
    xm_ref[...] = xm
    scores = jax.nn.sigmoid(_split_dot(xm, wr_ref[...]))
    sel = scores + br_ref[...]
    tm, ne = sel.shape
    per_group = ne // N_EXPERT_GROUPS
    lane = lax.broadcasted_iota(I32, (tm, ne), 1)
    grp = lane // per_group
    neg = -jnp.inf
    gs = []
    for g in range(N_EXPERT_GROUPS):
        m = jnp.where(grp == g, sel, neg)
        m1, pick = _first_argmax(m, lane, ne)
        m2 = jnp.max(jnp.where(pick, neg, m), axis=1, keepdims=True)
        gs.append(m1 + m2)
    allowed = jnp.zeros((tm, ne), F32)
    for g in range(N_EXPERT_GROUPS):
        rank = jnp.zeros((tm, 1), I32)
        for g2 in range(N_EXPERT_GROUPS):
            if g2 == g:
                continue
            ahead = gs[g2] > gs[g]
            if g2 < g:
                ahead = jnp.logical_or(ahead, gs[g2] == gs[g])
            rank = rank + ahead.astype(I32)
        allowed = allowed + jnp.where(grp == g, (rank < TOPK_GROUPS).astype(F32), 0.0)
    ms = jnp.where(allowed > 0.0, sel, neg)
    chosen = jnp.zeros((tm, ne), jnp.bool_)
    for _ in range(TOP_K):
        _, pick = _first_argmax(ms, lane, ne)
        chosen = jnp.logical_or(chosen, pick)
        ms = jnp.where(pick, neg, ms)
    tw = jnp.where(chosen, scores, 0.0)
    cw_ref[...] = tw / jnp.sum(tw, axis=1, keepdims=True) * ROUTED_SCALE
    mk_ref[...] = chosen.astype(BF16)

    @pl.when(i == 0)
    def _():
        cnt_ref[...] = jnp.zeros_like(cnt_ref)

    cnt_ref[...] += jnp.sum(chosen.astype(F32), axis=0, keepdims=True)


def _router(x, mods, layer, wr, br):
    t, d = x.shape
    ne = wr.shape[1]
    tm = 256
    ng = tm // GROUP
    return pl.pallas_call(
        _router_kernel,
        out_shape=(jax.ShapeDtypeStruct((t, d), F32), jax.ShapeDtypeStruct((t, ne), F32),
                   jax.ShapeDtypeStruct((t, ne), BF16), jax.ShapeDtypeStruct((1, ne), F32)),
        grid=(t // tm,),
        in_specs=[
            pl.BlockSpec((tm, d), lambda i: (i, 0)),
            pl.BlockSpec((1, ng, 1, d), lambda i: (layer, i, 0, 0)),
            pl.BlockSpec((1, ng, 1, d), lambda i: (layer, i, 0, 1)),
            pl.BlockSpec((d, ne), lambda i: (0, 0)),
            pl.BlockSpec((1, ne), lambda i: (0, 0)),
        ],
        out_specs=(pl.BlockSpec((tm, d), lambda i: (i, 0)), pl.BlockSpec((tm, ne), lambda i: (i, 0)),
                   pl.BlockSpec((tm, ne), lambda i: (i, 0)), pl.BlockSpec((1, ne), lambda i: (0, 0))),
        compiler_params=_cp(32, ("arbitrary",)),
        name="router",
    )(x, mods, mods, wr, br)


def _plan_kernel(mk_ref, cw_ref, start_ref, ltri_ref, upper_ref, pos_ref, w_ref, carry_ref):
    i = pl.program_id(0)

    @pl.when(i == 0)
    def _():
        carry_ref[...] = jnp.zeros_like(carry_ref)

    mk = mk_ref[...]
    rank = jnp.dot(ltri_ref[...], mk, preferred_element_type=F32)
    pos = start_ref[...] + carry_ref[...] + rank
    order = jnp.dot(mk, upper_ref[...], preferred_element_type=F32)
    chosen = mk > 0
    cw = cw_ref[...]
    for k in range(TOP_K):
        pick = jnp.logical_and(chosen, order == k)
        pos_ref[:, k:k + 1] = jnp.sum(jnp.where(pick, pos, 0.0), axis=1, keepdims=True).astype(I32)
        w_ref[:, k:k + 1] = jnp.sum(jnp.where(pick, cw, 0.0), axis=1, keepdims=True)
    carry_ref[...] += jnp.sum(mk.astype(F32), axis=0, keepdims=True)


def _plan(mk, cw, start):
    t, ne = mk.shape
    tm = 256
    r = jnp.arange(ne)
    upper = (r[:, None] < r[None, :]).astype(BF16)
    return pl.pallas_call(
        _plan_kernel,
        out_shape=(jax.ShapeDtypeStruct((t, TOP_K), I32), jax.ShapeDtypeStruct((t, TOP_K), F32)),
        grid=(t // tm,),
        in_specs=[
            pl.BlockSpec((tm, ne), lambda i: (i, 0)),
            pl.BlockSpec((tm, ne), lambda i: (i, 0)),
            pl.BlockSpec((1, ne), lambda i: (0, 0)),
            pl.BlockSpec((tm, tm), lambda i: (0, 0)),
            pl.BlockSpec((ne, ne), lambda i: (0, 0)),
        ],
        out_specs=(pl.BlockSpec((tm, TOP_K), lambda i: (i, 0)), pl.BlockSpec((tm, TOP_K), lambda i: (i, 0))),
        scratch_shapes=[pltpu.VMEM((1, ne), F32)],
        compiler_params=_cp(32, ("arbitrary",)),
        name="plan",
    )(mk, cw, start, _strict_lower(tm), upper)


def _row_copy(src_ref, src_row, dst_ref, dst_row, sem):
    return pltpu.make_async_copy(src_ref.at[pl.ds(src_row, 1)], dst_ref.at[pl.ds(dst_row, 1)], sem)


def _dispatch_kernel(fill_ref, pos_ref, xm_ref, xs_ref, zeros_ref, sem):
    i = pl.program_id(0)
    tm = xm_ref.shape[0]
    tile = zeros_ref.shape[0]
    ne = fill_ref.shape[0]

    def fill_copy(e):
        first = pl.multiple_of(fill_ref[e], 8)
        return pltpu.make_async_copy(zeros_ref, xs_ref.at[pl.ds(first, tile)], sem)

    @pl.when(i == 0)
    def _():
        zeros_ref[...] = jnp.zeros_like(zeros_ref)

        def start(e, c):
            fill_copy(e).start()
            return c

        def wait(e, c):
            fill_copy(e).wait()
            return c

        lax.fori_loop(0, ne, start, 0)
        lax.fori_loop(0, ne, wait, 0)

    def start(r, c):
        for k in range(TOP_K):
            _row_copy(xm_ref, r, xs_ref, pos_ref[r * TOP_K + k], sem).start()
        return c

    def wait(r, c):
        for k in range(TOP_K):
            _row_copy(xm_ref, r, xs_ref, pos_ref[r * TOP_K + k], sem).wait()
        return c

    lax.fori_loop(0, tm, start, 0)
    lax.fori_loop(0, tm, wait, 0)


def _dispatch(fill, pos_flat, xm, n_rows):
    t, d = xm.shape
    tm = 256
    return pl.pallas_call(
        _dispatch_kernel,
        out_shape=jax.ShapeDtypeStruct((n_rows, d), F32),
        grid_spec=pltpu.PrefetchScalarGridSpec(
            num_scalar_prefetch=1,
            grid=(t // tm,),
            in_specs=[
                pl.BlockSpec((tm * TOP_K,), lambda i, fill: (i,), memory_space=pltpu.SMEM),
                pl.BlockSpec((tm, d), lambda i, fill: (i, 0)),
            ],
            out_specs=pl.BlockSpec(memory_space=pl.ANY),
            scratch_shapes=[pltpu.VMEM((EXPERT_TILE, d), F32), pltpu.SemaphoreType.DMA(())],
        ),
        compiler_params=_cp(32, ("arbitrary",)),
        name="dispatch",
    )(fill, pos_flat, xm)


def _gmm_kernel(te_ref, nt_ref, xs_ref, wg_ref, wu_ref, wd_ref, ys_ref, wgb_ref, wub_ref, wdb_ref):
    i = pl.program_id(0)
    e = te_ref[i]
    prev = te_ref[jnp.maximum(i - 1, 0)]

    @pl.when(jnp.logical_or(i == 0, e != prev))
    def _():
        wgb_ref[...] = wg_ref[0, 0].astype(BF16)
        wub_ref[...] = wu_ref[0, 0].astype(BF16)
        wdb_ref[...] = wd_ref[0, 0].astype(BF16)

    @pl.when(i < nt_ref[0])
    def _():
        x = xs_ref[...].astype(BF16)
        hg = jnp.dot(x, wgb_ref[...], preferred_element_type=F32)
        hu = jnp.dot(x, wub_ref[...], preferred_element_type=F32)
        hid = (jax.nn.silu(hg) * hu).astype(BF16)
        ys_ref[...] = jnp.dot(hid, wdb_ref[...], preferred_element_type=F32)


def _gmm(tile_expert, n_tiles_used, xs, wg, wu, wd, layer, max_tiles):
    d = xs.shape[1]
    f = wg.shape[3]
    tm = EXPERT_TILE
    row = lambda i, te, nt: (jnp.minimum(i, nt[0] - 1), 0)
    return pl.pallas_call(
        _gmm_kernel,
        out_shape=jax.ShapeDtypeStruct(xs.shape, F32),
        grid_spec=pltpu.PrefetchScalarGridSpec(
            num_scalar_prefetch=2,
            grid=(max_tiles,),
            in_specs=[
                pl.BlockSpec((tm, d), row),
                pl.BlockSpec((1, 1, d, f), lambda i, te, nt: (layer, te[i], 0, 0)),
                pl.BlockSpec((1, 1, d, f), lambda i, te, nt: (layer, te[i], 0, 0)),
                pl.BlockSpec((1, 1, f, d), lambda i, te, nt: (layer, te[i], 0, 0)),
            ],
            out_specs=pl.BlockSpec((tm, d), row),
            scratch_shapes=[pltpu.VMEM((d, f), BF16), pltpu.VMEM((d, f), BF16), pltpu.VMEM((f, d), BF16)],
        ),
        compiler_params=_cp(56, ("arbitrary",)),
        name="gmm",
    )(tile_expert, n_tiles_used, xs, wg, wu, wd)


def _combine_kernel(pos_ref, w_ref, xm_ref, x_ref, gate_ref, wsg_ref, wsu_ref, wsd_ref,
                    g_ref, b_ref, ys_ref, o_ref, buf_ref, sem):
    tm = x_ref.shape[0]

    def copy(r, k):
        return pltpu.make_async_copy(ys_ref.at[pl.ds(pos_ref[r * TOP_K + k], 1)],
                                     buf_ref.at[k, pl.ds(r, 1)], sem)

    def start(r, c):
        for k in range(TOP_K):
            copy(r, k).start()
        return c

    def wait(r, c):
        for k in range(TOP_K):
            copy(r, k).wait()
        return c

    lax.fori_loop(0, tm, start, 0)
    xb = xm_ref[...].astype(BF16)
    hg = jnp.dot(xb, wsg_ref[...], preferred_element_type=F32)
    hu = jnp.dot(xb, wsu_ref[...], preferred_element_type=F32)
    f = jnp.dot((jax.nn.silu(hg) * hu).astype(BF16), wsd_ref[...], preferred_element_type=F32)
    lax.fori_loop(0, tm, wait, 0)
    w = w_ref[...]
    for k in range(TOP_K):
        f = f + w[:, k:k + 1] * buf_ref[k]
    o_ref[...] = _residual_ln(x_ref[...], f, gate_ref, g_ref, b_ref)


def _combine(pos_flat, w, xm, x, mods, layer, wsg, wsu, wsd, g, b, ys):
    t, d = x.shape
    fs = wsg.shape[1]
    tm = 128
    ng = tm // GROUP
    return pl.pallas_call(
        _combine_kernel,
        out_shape=jax.ShapeDtypeStruct((t, d), F32),
        grid=(t // tm,),
        in_specs=[
            pl.BlockSpec((tm * TOP_K,), lambda i: (i,), memory_space=pltpu.SMEM),
            pl.BlockSpec((tm, TOP_K), lambda i: (i, 0)),
            pl.BlockSpec((tm, d), lambda i: (i, 0)),
            pl.BlockSpec((tm, d), lambda i: (i, 0)),
            pl.BlockSpec((1, ng, 1, d), lambda i: (layer, i, 0, 2)),
            pl.BlockSpec((d, fs), lambda i: (0, 0)),
            pl.BlockSpec((d, fs), lambda i: (0, 0)),
            pl.BlockSpec((fs, d), lambda i: (0, 0)),
            pl.BlockSpec((1, d), lambda i: (0, 0)),
            pl.BlockSpec((1, d), lambda i: (0, 0)),
            pl.BlockSpec(memory_space=pl.ANY),
        ],
        out_specs=pl.BlockSpec((tm, d), lambda i: (i, 0)),
        scratch_shapes=[pltpu.VMEM((TOP_K, tm, d), F32), pltpu.SemaphoreType.DMA(())],
        compiler_params=_cp(48, ("arbitrary",)),
        name="combine",
    )(pos_flat, w, xm, x, mods, wsg, wsu, wsd, g, b, ys)


def _moe(x, mods, layer, wr, br, wg, wu, wd, wsg, wsu, wsd, g, b):
    t = x.shape[0]
    ne = wr.shape[1]
    tile = EXPERT_TILE
    max_tiles = (t * TOP_K) // tile + ne
    xm, cw, mk, cnt = _router(x, mods, layer, wr, br)
    cnt = cnt[0].astype(I32)
    tiles = (cnt + tile - 1) // tile
    ends = jnp.cumsum(tiles)
    start = (ends - tiles) * tile
    n_used = ends[-1:]
    tile_expert = jnp.searchsorted(ends, jnp.arange(max_tiles, dtype=I32), side="right")
    last_expert = jnp.searchsorted(ends, n_used[0] - 1, side="right")
    tile_expert = jnp.minimum(tile_expert, last_expert).astype(I32)
    pos, w = _plan(mk, cw, start.astype(F32)[None, :])
    pos_flat = pos.reshape(-1)
    xs = _dispatch((start + cnt) // 8 * 8, pos_flat, xm, (max_tiles + 1) * tile)
    ys = _gmm(tile_expert, n_used.astype(I32), xs, wg, wu, wd, layer, max_tiles)
    return _combine(pos_flat, w, xm, x, mods, layer, wsg, wsu, wsd, g, b, ys)


def _expand_mods(mod, n_prompt_groups, n_streams):
    depth, _, n = mod.shape
    p = jnp.broadcast_to(mod[:, 0:1], (depth, n_prompt_groups, n))
    return jnp.concatenate([p, mod[:, 1:1 + n_streams]], axis=1)[:, :, None, :]


def kernel(x_prompt, x_sample, c_prompt, c_sample, cache_k, cache_v, w_ada_mix, b_ada_mix, w_ada_ffn, b_ada_ffn, ln_mix_g, ln_mix_b, ln_ffn_g, ln_ffn_b, w_qkv, w_o, w_uv, b_uv, ln_v_g, ln_v_b, w_s, b_s, w_cm_out, w_router, b_router, w_gate, w_up, w_down, ws_gate, ws_up, ws_down):
    bp, sp, d = x_prompt.shape
    bs, t_new, _ = x_sample.shape
    assert bp == 1 and t_new == GROUP and sp % ATT_BLOCK == 0 and d == N_HEADS * HEAD_DIM
    n_prompt = bp * sp
    n_sample = bs * t_new
    width = w_uv.shape[2] // 2
    past = cache_k.shape[2]

    x = jnp.concatenate([x_prompt.reshape(n_prompt, d), x_sample.reshape(n_sample, d)], axis=0)

    rows = 8 * ((1 + bs + 7) // 8)
    c_all = jnp.zeros((rows, d), F32).at[0:1].set(c_prompt).at[1:1 + bs].set(c_sample)
    mods_mix = _expand_mods(_ada(c_all, w_ada_mix, b_ada_mix), n_prompt // GROUP, bs)
    mods_ffn = _expand_mods(_ada(c_all, w_ada_ffn, b_ada_ffn), n_prompt // GROUP, bs)

    cpos = jnp.arange(CM_CHUNK)
    cmask = (cpos[None, :] // CHUNK) <= (cpos[:, None] // CHUNK)
    pair = (cpos[None, :] // t_new) == (cpos[:, None] // t_new)
    fold = cpos % t_new
    smask = (fold[None, :] // CHUNK) <= (fold[:, None] // CHUNK)
    ws_prompt = jnp.where(cmask, w_s, 0.0)
    ws_sample = jnp.where(jnp.logical_and(pair, smask), w_s[:, :, fold][:, :, :, fold], 0.0)
    mix = jnp.stack([ws_prompt, ws_sample], axis=1).astype(BF16)
    gw = width // CM_GROUPS
    bias_p = jnp.repeat(jnp.swapaxes(b_s, 1, 2), gw, axis=2)
    mix_bias = jnp.stack([bias_p, bias_p[:, fold]], axis=1)

    ck = cache_k.reshape(cache_k.shape[0], bs, past * N_HEADS, HEAD_DIM)
    cv = cache_v.reshape(cache_v.shape[0], bs, past * N_HEADS, HEAD_DIM)

    ks, vs, cmv = [], [], []
    for i in range(DEPTH):
        j = i // 2
        if i % 2 == 0:
            qkv = _modmm(x, mods_mix, i, w_qkv[j].astype(BF16), jnp.zeros((1, 3 * d), F32), F32, False)
            o_p = _attn_prompt(qkv, n_prompt)
            o_s, cs = _attn_sample(qkv, ck, cv, j, n_prompt, bs, t_new)
            o_s = lax.cond(
                jnp.max(cs) > STICK_DONE,
                lambda: _attn_sample_rest(qkv, o_s, cs, ck, cv, j, n_prompt, bs, t_new),
                lambda: o_s)
            o = jnp.concatenate([o_p, o_s.astype(BF16)], axis=0)
            x = _proj_ln(o, x, mods_mix, i, w_o[j].astype(BF16), ln_mix_g[i][None], ln_mix_b[i][None])
            ks.append(qkv[:, d:2 * d])
            vs.append(qkv[:, 2 * d:])
        else:
            zz = _modmm(x, mods_mix, i, w_uv[j].astype(BF16), b_uv[j][None], BF16, True)
            x, v = _cm_mix(zz, x, mods_mix, i, mix[j], mix_bias[j], ln_v_g[j][None], ln_v_b[j][None],
                           w_cm_out[j].astype(BF16), ln_mix_g[i][None], ln_mix_b[i][None], n_prompt)
            cmv.append(v)
        x = _moe(x, mods_ffn, i, w_router[i], b_router[i][None], w_gate, w_up, w_down,
                 ws_gate[i].astype(BF16), ws_up[i].astype(BF16), ws_down[i].astype(BF16),
                 ln_ffn_g[i][None], ln_ffn_b[i][None])

    def heads(a, lo, hi, b):
        return jnp.stack([m[lo:hi].reshape(b, -1, N_HEADS, HEAD_DIM) for m in a])

    return (
        x[:n_prompt].reshape(bp, sp, d),
        x[n_prompt:].reshape(bs, t_new, d),
        heads(ks, 0, n_prompt, bp),
        heads(vs, 0, n_prompt, bp),
        heads(ks, n_prompt, n_prompt + n_sample, bs),
        heads(vs, n_prompt, n_prompt + n_sample, bs),
        jnp.stack([v[n_prompt - CM_CHUNK:n_prompt].reshape(bp, CM_CHUNK, width) for v in cmv]),
        jnp.stack([v[n_prompt:].reshape(bs, t_new, width) for v in cmv]),
    )
```

```python
import functools

import jax
import jax.numpy as jnp
from jax import lax
from jax.experimental import pallas as pl
from jax.experimental.pallas import tpu as pltpu

F32 = jnp.float32
BF16 = jnp.bfloat16
I32 = jnp.int32

DEPTH = 4
N_HEADS = 16
HEAD_DIM = 128
CHUNK = 64
CM_CHUNK = 128
CM_GROUPS = 16
N_EXPERTS = 64
N_EXPERT_GROUPS = 8
TOPK_GROUPS = 4
TOP_K = 8
ROUTED_SCALE = 2.5
ALPHA = (2 * DEPTH) ** 0.25
LN_EPS = 1e-5

GROUP = 64
MIB = 1024 * 1024
STICK_DONE = -104.0
ATT_BLOCK = 256
EXPERT_TILE = 256


def _cp(vmem_mib, sem):
    return pltpu.CompilerParams(dimension_semantics=sem, vmem_limit_bytes=vmem_mib * MIB)


def _ln(y, g, b):
    mu = jnp.mean(y, axis=-1, keepdims=True)
    yc = y - mu
    var = jnp.mean(yc * yc, axis=-1, keepdims=True)
    return yc * lax.rsqrt(var + LN_EPS) * g + b


def _modulate(x, sh_ref, sc_ref):
    tm, d = x.shape
    g = sh_ref.shape[1]
    xg = x.reshape(g, tm // g, d)
    return (xg * sc_ref[0] + sh_ref[0]).reshape(tm, d)


def _residual_ln(x, f, gate_ref, g_ref, b_ref):
    tm, d = x.shape
    g = gate_ref.shape[1]
    y = ALPHA * x + (gate_ref[0] * f.reshape(g, tm // g, d)).reshape(tm, d)
    return _ln(y, g_ref[...], b_ref[...])


TOKEN_ROWS = 8
LANES = 128
U32 = jnp.uint32


def _pack_halves(y):
    n = y.shape[1] // 2
    lo = lax.bitcast_convert_type(y[:, :n].astype(BF16).astype(F32), U32)
    hi = lax.bitcast_convert_type(y[:, n:].astype(BF16).astype(F32), U32)
    return hi | (lo >> 16)


def _unpack_halves(w):
    lo = lax.bitcast_convert_type(w << 16, F32)
    hi = lax.bitcast_convert_type(w & jnp.uint32(0xFFFF0000), F32)
    return lo, hi


def _store_token_tiles(ref, words):
    tm = words.shape[0]
    for c in range(TOKEN_ROWS):
        ref[pl.ds(c, tm, stride=TOKEN_ROWS), :] = words[:, c * LANES:(c + 1) * LANES]


def _load_token_tiles(ref):
    tm = ref.shape[0] // TOKEN_ROWS
    return jnp.concatenate([ref[pl.ds(c, tm, stride=TOKEN_ROWS), :] for c in range(TOKEN_ROWS)], axis=1)


def _token_tile(ref, i):
    return ref.at[pl.ds(pl.multiple_of(i * TOKEN_ROWS, TOKEN_ROWS), TOKEN_ROWS)]


def _ada_kernel(c_ref, w_ref, b_ref, o_ref, *, d_model, tn):
    j = pl.program_id(1)
    a = jax.nn.silu(c_ref[...]).astype(BF16)
    acc = jnp.dot(a, w_ref[0].astype(BF16), preferred_element_type=F32) + b_ref[0]
    o_ref[0] = acc + jnp.where(j * tn >= d_model, 1.0, 0.0).astype(F32)


def _ada(c_all, w, b):
    depth, d, n = w.shape
    r = c_all.shape[0]
    tn = 1024
    return pl.pallas_call(
        functools.partial(_ada_kernel, d_model=d, tn=tn),
        out_shape=jax.ShapeDtypeStruct((depth, r, n), F32),
        grid=(depth, n // tn),
        in_specs=[
            pl.BlockSpec((r, d), lambda l, j: (0, 0)),
            pl.BlockSpec((1, d, tn), lambda l, j: (l, 0, j)),
            pl.BlockSpec((1, 1, tn), lambda l, j: (l, 0, j)),
        ],
        out_specs=pl.BlockSpec((1, r, tn), lambda l, j: (l, 0, j)),
        compiler_params=_cp(40, ("parallel", "parallel")),
        name="ada",
    )(c_all, w, b.reshape(depth, 1, n))


def _modmm_kernel(x_ref, sh_ref, sc_ref, w_ref, b_ref, o_ref, *, gelu):
    h = _modulate(x_ref[...], sh_ref, sc_ref).astype(BF16)
    acc = jnp.dot(h, w_ref[...], preferred_element_type=F32) + b_ref[...]
    if gelu:
        acc = jax.nn.gelu(acc)
    o_ref[...] = acc.astype(o_ref.dtype)


def _modmm(x, mods, layer, w, b, out_dtype, gelu):
    t, d = x.shape
    n = w.shape[1]
    tm, tn = 512, 2048
    g = tm // GROUP
    return pl.pallas_call(
        functools.partial(_modmm_kernel, gelu=gelu),
        out_shape=jax.ShapeDtypeStruct((t, n), out_dtype),
        grid=(n // tn, t // tm),
        in_specs=[
            pl.BlockSpec((tm, d), lambda j, i: (i, 0)),
            pl.BlockSpec((1, g, 1, d), lambda j, i: (layer, i, 0, 0)),
            pl.BlockSpec((1, g, 1, d), lambda j, i: (layer, i, 0, 1)),
            pl.BlockSpec((d, tn), lambda j, i: (0, j)),
            pl.BlockSpec((1, tn), lambda j, i: (0, j)),
        ],
        out_specs=pl.BlockSpec((tm, tn), lambda j, i: (i, j)),
        compiler_params=_cp(48, ("parallel", "parallel")),
        name="modmm",
    )(x, mods, mods, w, b)


def _sb_block(q, k, v, csum, u, mask):
    z = lax.dot_general(q, k, (((1,), (1,)), ((), ())), preferred_element_type=F32)
    z = z * (HEAD_DIM ** -0.5)
    t = jnp.log1p(jnp.exp(-jnp.abs(z)))
    log_beta = -(jnp.maximum(-z, 0.0) + t)
    l1 = -(jnp.maximum(z, 0.0) + t)
    if mask is not None:
        l1 = jnp.where(mask, l1, 0.0)
    hi = l1.astype(BF16)
    lo = (l1 - hi.astype(F32)).astype(BF16)
    s = jnp.dot(hi, u, preferred_element_type=F32) + jnp.dot(lo, u, preferred_element_type=F32)
    a = jnp.exp(log_beta + s + csum)
    if mask is not None:
        a = jnp.where(mask, a, 0.0)
    o = jnp.dot(a.astype(BF16), v, preferred_element_type=F32)
    return o, csum + s[:, :1] + l1[:, :1]


def _causal_mask(n):
    row = lax.broadcasted_iota(I32, (n, n), 0)
    col = lax.broadcasted_iota(I32, (n, n), 1)
    return col < row


def _attn_prompt_kernel(q_ref, k_ref, v_ref, u_ref, o_ref):
    i = pl.program_id(1)
    bq = q_ref.shape[0]
    heads = q_ref.shape[1] // HEAD_DIM
    u = u_ref[...]
    mask = _causal_mask(bq)

    def kv(b, cols):
        start = pl.multiple_of(b * bq, bq)
        return (k_ref[pl.ds(start, bq), cols].astype(BF16), v_ref[pl.ds(start, bq), cols].astype(BF16))

    prev = jnp.maximum(i - 1, 0)
    has_prev = i > 0
    qs, os, css = [], [], []
    for h in range(heads):
        cols = slice(h * HEAD_DIM, (h + 1) * HEAD_DIM)
        q = q_ref[:, cols].astype(BF16)
        k0, v0 = kv(i, cols)
        o, cs = _sb_block(q, k0, v0, jnp.zeros((bq, 1), F32), u, mask)
        k1, v1 = kv(prev, cols)
        do, cs1 = _sb_block(q, k1, v1, cs, u, None)
        qs.append(q)
        os.append(o + jnp.where(has_prev, do, 0.0))
        css.append(jnp.where(has_prev, cs1, cs))

    def cond(c):
        b, _, css = c
        live = jnp.max(css[0])
        for cs in css[1:]:
            live = jnp.maximum(live, jnp.max(cs))
        return jnp.logical_and(b >= 0, live > STICK_DONE)

    def body(c):
        b, os, css = c
        new_os, new_css = [], []
        for h in range(heads):
            cols = slice(h * HEAD_DIM, (h + 1) * HEAD_DIM)
            kb, vb = kv(b, cols)
            do, cs = _sb_block(qs[h], kb, vb, css[h], u, None)
            new_os.append(os[h] + do)
            new_css.append(cs)
        return b - 1, tuple(new_os), tuple(new_css)

    _, os, _ = lax.while_loop(cond, body, (i - 2, tuple(os), tuple(css)))
    for h in range(heads):
        o_ref[:, h * HEAD_DIM:(h + 1) * HEAD_DIM] = os[h].astype(o_ref.dtype)


def _strict_lower(n):
    r = jnp.arange(n)
    return (r[:, None] > r[None, :]).astype(BF16)


def _attn_prompt(qkv, n_prompt):
    d = N_HEADS * HEAD_DIM
    bq = ATT_BLOCK
    hp = 2
    groups = N_HEADS // hp
    wcols = hp * HEAD_DIM
    return pl.pallas_call(
        _attn_prompt_kernel,
        out_shape=jax.ShapeDtypeStruct((n_prompt, d), BF16),
        grid=(groups, n_prompt // bq),
        in_specs=[
            pl.BlockSpec((bq, wcols), lambda h, i: (i, h)),
            pl.BlockSpec((n_prompt, wcols), lambda h, i: (0, groups + h)),
            pl.BlockSpec((n_prompt, wcols), lambda h, i: (0, 2 * groups + h)),
            pl.BlockSpec((bq, bq), lambda h, i: (0, 0)),
        ],
        out_specs=pl.BlockSpec((bq, wcols), lambda h, i: (i, h)),
        compiler_params=_cp(48, ("parallel", "parallel")),
        name="attn_prompt",
    )(qkv, qkv, qkv, _strict_lower(bq))


def _attn_sample_kernel(q_ref, kn_ref, vn_ref, ck_ref, cv_ref, un_ref, uc_ref, o_ref, cs_ref):
    tq = q_ref.shape[0]
    bk = uc_ref.shape[0]
    mask = _causal_mask(tq)
    for h in range(N_HEADS):
        cols = slice(h * HEAD_DIM, (h + 1) * HEAD_DIM)
        q = q_ref[:, cols].astype(BF16)
        o, cs = _sb_block(q, kn_ref[:, cols].astype(BF16), vn_ref[:, cols].astype(BF16),
                          jnp.zeros((tq, 1), F32), un_ref[...], mask)
        kc = ck_ref[0, 0, pl.ds(h, bk, stride=N_HEADS), :].astype(BF16)
        vc = cv_ref[0, 0, pl.ds(h, bk, stride=N_HEADS), :].astype(BF16)
        do, cs = _sb_block(q, kc, vc, cs, uc_ref[...], None)
        o_ref[:, cols] = o + do
        cs_ref[:, cols] = jnp.broadcast_to(cs, (tq, HEAD_DIM))


def _attn_sample(qkv, cache_k, cache_v, layer, n_prompt, n_streams, t_new):
    d = N_HEADS * HEAD_DIM
    past = cache_k.shape[2] // N_HEADS
    bk = ATT_BLOCK
    pb = n_prompt // t_new
    last = past // bk - 1
    return pl.pallas_call(
        _attn_sample_kernel,
        out_shape=(jax.ShapeDtypeStruct((n_streams * t_new, d), F32),
                   jax.ShapeDtypeStruct((n_streams * t_new, d), F32)),
        grid=(n_streams,),
        in_specs=[
            pl.BlockSpec((t_new, d), lambda s: (pb + s, 0)),
            pl.BlockSpec((t_new, d), lambda s: (pb + s, 1)),
            pl.BlockSpec((t_new, d), lambda s: (pb + s, 2)),
            pl.BlockSpec((1, 1, bk * N_HEADS, HEAD_DIM), lambda s: (layer, s, last, 0)),
            pl.BlockSpec((1, 1, bk * N_HEADS, HEAD_DIM), lambda s: (layer, s, last, 0)),
            pl.BlockSpec((t_new, t_new), lambda s: (0, 0)),
            pl.BlockSpec((bk, bk), lambda s: (0, 0)),
        ],
        out_specs=(pl.BlockSpec((t_new, d), lambda s: (s, 0)),
                   pl.BlockSpec((t_new, d), lambda s: (s, 0))),
        compiler_params=_cp(32, ("parallel",)),
        name="attn_sample",
    )(qkv, qkv, qkv, cache_k, cache_v, _strict_lower(t_new), _strict_lower(bk))


def _attn_sample_rest_kernel(q_ref, oin_ref, csin_ref, ck_ref, cv_ref, u_ref, o_ref, cs_ref):
    b = pl.program_id(1)
    bk = u_ref.shape[0]

    @pl.when(b == 0)
    def _():
        o_ref[...] = oin_ref[...]
        cs_ref[...] = csin_ref[...]

    for h in range(N_HEADS):
        cols = slice(h * HEAD_DIM, (h + 1) * HEAD_DIM)
        cs = cs_ref[:, cols][:, :1]

        @pl.when(jnp.max(cs) > STICK_DONE)
        def _(h=h, cols=cols, cs=cs):
            q = q_ref[:, cols].astype(BF16)
            kc = ck_ref[0, 0, pl.ds(h, bk, stride=N_HEADS), :].astype(BF16)
            vc = cv_ref[0, 0, pl.ds(h, bk, stride=N_HEADS), :].astype(BF16)
            do, cs2 = _sb_block(q, kc, vc, cs, u_ref[...], None)
            o_ref[:, cols] += do
            cs_ref[:, cols] = jnp.broadcast_to(cs2, (cs2.shape[0], HEAD_DIM))


def _attn_sample_rest(qkv, o_part, cs_part, cache_k, cache_v, layer, n_prompt, n_streams, t_new):
    d = N_HEADS * HEAD_DIM
    past = cache_k.shape[2] // N_HEADS
    bk = ATT_BLOCK
    pb = n_prompt // t_new
    nb = past // bk - 1
    o, _ = pl.pallas_call(
        _attn_sample_rest_kernel,
        out_shape=(jax.ShapeDtypeStruct((n_streams * t_new, d), F32),
                   jax.ShapeDtypeStruct((n_streams * t_new, d), F32)),
        grid=(n_streams, nb),
        in_specs=[
            pl.BlockSpec((t_new, d), lambda s, b: (pb + s, 0)),
            pl.BlockSpec((t_new, d), lambda s, b: (s, 0)),
            pl.BlockSpec((t_new, d), lambda s, b: (s, 0)),
            pl.BlockSpec((1, 1, bk * N_HEADS, HEAD_DIM), lambda s, b: (layer, s, nb - 1 - b, 0)),
            pl.BlockSpec((1, 1, bk * N_HEADS, HEAD_DIM), lambda s, b: (layer, s, nb - 1 - b, 0)),
            pl.BlockSpec((bk, bk), lambda s, b: (0, 0)),
        ],
        out_specs=(pl.BlockSpec((t_new, d), lambda s, b: (s, 0)),
                   pl.BlockSpec((t_new, d), lambda s, b: (s, 0))),
        compiler_params=_cp(32, ("parallel", "arbitrary")),
        name="attn_sample_rest",
    )(qkv, o_part, cs_part, cache_k, cache_v, _strict_lower(bk))
    return o


def _proj_ln_kernel(a_ref, x_ref, gate_ref, w_ref, g_ref, b_ref, o_ref):
    f = jnp.dot(a_ref[...].astype(BF16), w_ref[...], preferred_element_type=F32)
    o_ref[...] = _residual_ln(x_ref[...], f, gate_ref, g_ref, b_ref)


def _proj_ln(a, x, mods, layer, w, g, b):
    t, d = x.shape
    kdim = a.shape[1]
    tm = 256
    ng = tm // GROUP
    return pl.pallas_call(
        _proj_ln_kernel,
        out_shape=jax.ShapeDtypeStruct((t, d), F32),
        grid=(t // tm,),
        in_specs=[
            pl.BlockSpec((tm, kdim), lambda i: (i, 0)),
            pl.BlockSpec((tm, d), lambda i: (i, 0)),
            pl.BlockSpec((1, ng, 1, d), lambda i: (layer, i, 0, 2)),
            pl.BlockSpec((kdim, d), lambda i: (0, 0)),
            pl.BlockSpec((1, d), lambda i: (0, 0)),
            pl.BlockSpec((1, d), lambda i: (0, 0)),
        ],
        out_specs=pl.BlockSpec((tm, d), lambda i: (i, 0)),
        compiler_params=_cp(40, ("parallel",)),
        name="proj_ln",
    )(a, x, mods, w, g, b)


def _cm_mix_kernel(u_ref, vraw_ref, x_ref, gate_ref, mx_ref, bias_ref, lvg_ref, lvb_ref,
                   wo_ref, g_ref, b_ref, o_ref, v_ref, gated_ref):
    tm, width = u_ref.shape
    gw = width // CM_GROUPS
    v = _ln(vraw_ref[...].astype(F32), lvg_ref[...], lvb_ref[...])
    v_ref[...] = v
    vb = v.astype(BF16)
    for c in range(tm // CM_CHUNK):
        rows = slice(c * CM_CHUNK, (c + 1) * CM_CHUNK)
        for g in range(CM_GROUPS):
            cols = slice(g * gw, (g + 1) * gw)
            mixed = jnp.dot(mx_ref[0, g], vb[rows, cols], preferred_element_type=F32)
            mixed = mixed + bias_ref[0, :, cols]
            gated_ref[rows, cols] = (u_ref[rows, cols].astype(F32) * mixed).astype(BF16)
    f = jnp.dot(gated_ref[...], wo_ref[...], preferred_element_type=F32)
    o_ref[...] = _residual_ln(x_ref[...], f, gate_ref, g_ref, b_ref)


def _cm_mix(zz, x, mods, layer, mx, bias, lvg, lvb, wo, g, b, n_prompt):
    t, d = x.shape
    width = zz.shape[1] // 2
    tm = CM_CHUNK
    ng = tm // GROUP
    n_prompt_tiles = n_prompt // tm
    kind = lambda i: jnp.where(i < n_prompt_tiles, 0, 1)
    return pl.pallas_call(
        _cm_mix_kernel,
        out_shape=(jax.ShapeDtypeStruct((t, d), F32), jax.ShapeDtypeStruct((t, width), F32)),
        grid=(t // tm,),
        in_specs=[
            pl.BlockSpec((tm, width), lambda i: (i, 0)),
            pl.BlockSpec((tm, width), lambda i: (i, 1)),
            pl.BlockSpec((tm, d), lambda i: (i, 0)),
            pl.BlockSpec((1, ng, 1, d), lambda i: (layer, i, 0, 2)),
            pl.BlockSpec((1, CM_GROUPS, CM_CHUNK, CM_CHUNK), lambda i: (kind(i), 0, 0, 0)),
            pl.BlockSpec((1, CM_CHUNK, width), lambda i: (kind(i), 0, 0)),
            pl.BlockSpec((1, width), lambda i: (0, 0)),
            pl.BlockSpec((1, width), lambda i: (0, 0)),
            pl.BlockSpec((width, d), lambda i: (0, 0), pipeline_mode=pl.Buffered(1)),
            pl.BlockSpec((1, d), lambda i: (0, 0)),
            pl.BlockSpec((1, d), lambda i: (0, 0)),
        ],
        out_specs=(pl.BlockSpec((tm, d), lambda i: (i, 0)),
                   pl.BlockSpec((tm, width), lambda i: (i, 0))),
        scratch_shapes=[pltpu.VMEM((tm, width), BF16)],
        compiler_params=_cp(48, ("parallel",)),
        name="cm_mix",
    )(zz, zz, x, mods, mx, bias, lvg, lvb, wo, g, b)


def _split_dot(x, w):
    xh = x.astype(BF16)
    xl = (x - xh.astype(F32)).astype(BF16)
    wh = w.astype(BF16)
    wl = (w - wh.astype(F32)).astype(BF16)
    d = lambda a, b: jnp.dot(a, b, preferred_element_type=F32)
    return d(xh, wh) + (d(xh, wl) + d(xl, wh))


def _first_argmax(m, lane, n):
    m1 = jnp.max(m, axis=1, keepdims=True)
    i1 = jnp.min(jnp.where(m == m1, lane, n), axis=1, keepdims=True)
    return m1, lane == i1


def _router_kernel(x_ref, sh_ref, sc_ref, wr_ref, br_ref, xm_ref, cw_ref, mk_ref, cnt_ref):
    i = pl.program_id(0)
    xm = _modulate(x_ref[...], sh_ref, sc_ref)
    _store_token_tiles(xm_ref, _pack_halves(xm))
    scores = jax.nn.sigmoid(_split_dot(xm, wr_ref[...]))
    sel = scores + br_ref[...]
    tm, ne = sel.shape
    per_group = ne // N_EXPERT_GROUPS
    lane = lax.broadcasted_iota(I32, (tm, ne), 1)
    grp = lane // per_group
    neg = -jnp.inf
    gs = []
    for g in range(N_EXPERT_GROUPS):
        m = jnp.where(grp == g, sel, neg)
        m1, pick = _first_argmax(m, lane, ne)
        m2 = jnp.max(jnp.where(pick, neg, m), axis=1, keepdims=True)
        gs.append(m1 + m2)
    allowed = jnp.zeros((tm, ne), F32)
    for g in range(N_EXPERT_GROUPS):
        rank = jnp.zeros((tm, 1), I32)
        for g2 in range(N_EXPERT_GROUPS):
            if g2 == g:
                continue
            ahead = gs[g2] > gs[g]
            if g2 < g:
                ahead = jnp.logical_or(ahead, gs[g2] == gs[g])
            rank = rank + ahead.astype(I32)
        allowed = allowed + jnp.where(grp == g, (rank < TOPK_GROUPS).astype(F32), 0.0)
    ms = jnp.where(allowed > 0.0, sel, neg)
    chosen = jnp.zeros((tm, ne), jnp.bool_)
    for _ in range(TOP_K):
        _, pick = _first_argmax(ms, lane, ne)
        chosen = jnp.logical_or(chosen, pick)
        ms = jnp.where(pick, neg, ms)
    tw = jnp.where(chosen, scores, 0.0)
    cw_ref[...] = tw / jnp.sum(tw, axis=1, keepdims=True) * ROUTED_SCALE
    mk_ref[...] = chosen.astype(BF16)

    @pl.when(i == 0)
    def _():
        cnt_ref[...] = jnp.zeros_like(cnt_ref)

    cnt_ref[...] += jnp.sum(chosen.astype(F32), axis=0, keepdims=True)


def _router(x, mods, layer, wr, br):
    t, d = x.shape
    ne = wr.shape[1]
    tm = 256
    ng = tm // GROUP
    return pl.pallas_call(
        _router_kernel,
        out_shape=(jax.ShapeDtypeStruct((t * TOKEN_ROWS, LANES), U32), jax.ShapeDtypeStruct((t, ne), F32),
                   jax.ShapeDtypeStruct((t, ne), BF16), jax.ShapeDtypeStruct((1, ne), F32)),
        grid=(t // tm,),
        in_specs=[
            pl.BlockSpec((tm, d), lambda i: (i, 0)),
            pl.BlockSpec((1, ng, 1, d), lambda i: (layer, i, 0, 0)),
            pl.BlockSpec((1, ng, 1, d), lambda i: (layer, i, 0, 1)),
            pl.BlockSpec((d, ne), lambda i: (0, 0)),
            pl.BlockSpec((1, ne), lambda i: (0, 0)),
        ],
        out_specs=(pl.BlockSpec((tm * TOKEN_ROWS, LANES), lambda i: (i, 0)),
                   pl.BlockSpec((tm, ne), lambda i: (i, 0)),
                   pl.BlockSpec((tm, ne), lambda i: (i, 0)), pl.BlockSpec((1, ne), lambda i: (0, 0))),
        compiler_params=_cp(32, ("arbitrary",)),
        name="router",
    )(x, mods, mods, wr, br)


def _plan_kernel(mk_ref, cw_ref, start_ref, ltri_ref, upper_ref, pos_ref, w_ref, carry_ref):
    i = pl.program_id(0)

    @pl.when(i == 0)
    def _():
        carry_ref[...] = jnp.zeros_like(carry_ref)

    mk = mk_ref[...]
    rank = jnp.dot(ltri_ref[...], mk, preferred_element_type=F32)
    pos = start_ref[...] + carry_ref[...] + rank
    order = jnp.dot(mk, upper_ref[...], preferred_element_type=F32)
    chosen = mk > 0
    cw = cw_ref[...]
    for k in range(TOP_K):
        pick = jnp.logical_and(chosen, order == k)
        pos_ref[:, k:k + 1] = jnp.sum(jnp.where(pick, pos, 0.0), axis=1, keepdims=True).astype(I32)
        w_ref[:, k:k + 1] = jnp.sum(jnp.where(pick, cw, 0.0), axis=1, keepdims=True)
    carry_ref[...] += jnp.sum(mk.astype(F32), axis=0, keepdims=True)


def _plan(mk, cw, start):
    t, ne = mk.shape
    tm = 256
    r = jnp.arange(ne)
    upper = (r[:, None] < r[None, :]).astype(BF16)
    return pl.pallas_call(
        _plan_kernel,
        out_shape=(jax.ShapeDtypeStruct((t, TOP_K), I32), jax.ShapeDtypeStruct((t, TOP_K), F32)),
        grid=(t // tm,),
        in_specs=[
            pl.BlockSpec((tm, ne), lambda i: (i, 0)),
            pl.BlockSpec((tm, ne), lambda i: (i, 0)),
            pl.BlockSpec((1, ne), lambda i: (0, 0)),
            pl.BlockSpec((tm, tm), lambda i: (0, 0)),
            pl.BlockSpec((ne, ne), lambda i: (0, 0)),
        ],
        out_specs=(pl.BlockSpec((tm, TOP_K), lambda i: (i, 0)), pl.BlockSpec((tm, TOP_K), lambda i: (i, 0))),
        scratch_shapes=[pltpu.VMEM((1, ne), F32)],
        compiler_params=_cp(32, ("arbitrary",)),
        name="plan",
    )(mk, cw, start, _strict_lower(tm), upper)


def _dispatch_kernel(fill_ref, pad_ref, pos_ref, xm_ref, xs_ref, zeros_ref, sem):
    i = pl.program_id(0)
    tm = xm_ref.shape[0] // TOKEN_ROWS
    ne = fill_ref.shape[0]
    pad_bits = (zeros_ref.shape[0] // TOKEN_ROWS).bit_length() - 1

    def row_copy(r, k):
        return pltpu.make_async_copy(_token_tile(xm_ref, r), _token_tile(xs_ref, pos_ref[r * TOP_K + k]), sem)

    def fill(e, wait):
        first, n = fill_ref[e], pad_ref[e]
        for bit in range(pad_bits):
            size = (1 << bit) * TOKEN_ROWS
            done = (n >> (bit + 1)) << (bit + 1)
            dst = pl.multiple_of((first + done) * TOKEN_ROWS, TOKEN_ROWS)
            piece = pltpu.make_async_copy(zeros_ref.at[pl.ds(0, size)], xs_ref.at[pl.ds(dst, size)], sem)

            @pl.when(((n >> bit) & 1) == 1)
            def _(piece=piece):
                if wait:
                    piece.wait()
                else:
                    piece.start()

    @pl.when(i == 0)
    def _():
        zeros_ref[...] = jnp.zeros_like(zeros_ref)

        def start(e, c):
            fill(e, False)
            return c

        def wait(e, c):
            fill(e, True)
            return c

        lax.fori_loop(0, ne, start, 0)
        lax.fori_loop(0, ne, wait, 0)

    def start(r, c):
        for k in range(TOP_K):
            row_copy(r, k).start(priority=k % 2)
        return c

    def wait(r, c):
        for k in range(TOP_K):
            row_copy(r, k).wait()
        return c

    lax.fori_loop(0, tm, start, 0)
    lax.fori_loop(0, tm, wait, 0)


def _dispatch(fill, pad, pos_flat, xm, n_slots):
    t = xm.shape[0] // TOKEN_ROWS
    tm = 256
    return pl.pallas_call(
        _dispatch_kernel,
        out_shape=jax.ShapeDtypeStruct((n_slots * TOKEN_ROWS, LANES), U32),
        grid_spec=pltpu.PrefetchScalarGridSpec(
            num_scalar_prefetch=2,
            grid=(t // tm,),
            in_specs=[
                pl.BlockSpec((tm * TOP_K,), lambda i, fill, pad: (i,), memory_space=pltpu.SMEM),
                pl.BlockSpec((tm * TOKEN_ROWS, LANES), lambda i, fill, pad: (i, 0)),
            ],
            out_specs=pl.BlockSpec(memory_space=pl.ANY),
            scratch_shapes=[pltpu.VMEM((EXPERT_TILE * TOKEN_ROWS, LANES), U32), pltpu.SemaphoreType.DMA(())],
        ),
        compiler_params=_cp(32, ("arbitrary",)),
        name="dispatch",
    )(fill, pad, pos_flat, xm)


def _gmm_kernel(te_ref, nt_ref, xs_ref, wg_ref, wu_ref, wd_ref, ys_ref, wgb_ref, wub_ref, wdb_ref):
    i = pl.program_id(0)
    e = te_ref[i]
    prev = te_ref[jnp.maximum(i - 1, 0)]

    @pl.when(jnp.logical_or(i == 0, e != prev))
    def _():
        wgb_ref[...] = wg_ref[0, 0].astype(BF16)
        wub_ref[...] = wu_ref[0, 0].astype(BF16)
        wdb_ref[...] = wd_ref[0, 0].astype(BF16)

    @pl.when(i < nt_ref[0])
    def _():
        lo, hi = _unpack_halves(_load_token_tiles(xs_ref))
        x = jnp.concatenate([lo, hi], axis=1).astype(BF16)
        hg = jnp.dot(x, wgb_ref[...], preferred_element_type=F32)
        hu = jnp.dot(x, wub_ref[...], preferred_element_type=F32)
        hid = (jax.nn.silu(hg) * hu).astype(BF16)
        y = jnp.dot(hid, wdb_ref[...], preferred_element_type=F32)
        _store_token_tiles(ys_ref, _pack_halves(y))


def _gmm(tile_expert, n_tiles_used, xs, wg, wu, wd, layer, max_tiles):
    d, f = wg.shape[2], wg.shape[3]
    tm = EXPERT_TILE
    row = lambda i, te, nt: (jnp.minimum(i, nt[0] - 1), 0)
    return pl.pallas_call(
        _gmm_kernel,
        out_shape=jax.ShapeDtypeStruct(xs.shape, U32),
        grid_spec=pltpu.PrefetchScalarGridSpec(
            num_scalar_prefetch=2,
            grid=(max_tiles,),
            in_specs=[
                pl.BlockSpec((tm * TOKEN_ROWS, LANES), row),
                pl.BlockSpec((1, 1, d, f), lambda i, te, nt: (layer, te[i], 0, 0)),
                pl.BlockSpec((1, 1, d, f), lambda i, te, nt: (layer, te[i], 0, 0)),
                pl.BlockSpec((1, 1, f, d), lambda i, te, nt: (layer, te[i], 0, 0)),
            ],
            out_specs=pl.BlockSpec((tm * TOKEN_ROWS, LANES), row),
            scratch_shapes=[pltpu.VMEM((d, f), BF16), pltpu.VMEM((d, f), BF16), pltpu.VMEM((f, d), BF16)],
        ),
        compiler_params=_cp(56, ("arbitrary",)),
        name="gmm",
    )(tile_expert, n_tiles_used, xs, wg, wu, wd)


def _combine_kernel(pos_ref, nxt_ref, w_ref, xm_ref, x_ref, gate_ref, wsg_ref, wsu_ref, wsd_ref,
                    g_ref, b_ref, ys_ref, o_ref, buf_ref, sems):
    i = pl.program_id(0)
    n = pl.num_programs(0)
    tm, d = x_ref.shape
    half = d // 2
    slot = i % 2

    def copy(p_ref, s, r, k):
        return pltpu.make_async_copy(_token_tile(ys_ref, p_ref[r * TOP_K + k]),
                                     _token_tile(buf_ref.at[s, k], r), sems.at[s])

    def gather(p_ref, s):
        def start(r, c):
            for k in range(TOP_K):
                copy(p_ref, s, r, k).start(priority=k % 2)
            return c
        lax.fori_loop(0, tm, start, 0)

    @pl.when(i == 0)
    def _():
        gather(pos_ref, slot)

    @pl.when(i + 1 < n)
    def _():
        gather(nxt_ref, 1 - slot)

    lo, hi = _unpack_halves(_load_token_tiles(xm_ref))
    xb = jnp.concatenate([lo, hi], axis=1).astype(BF16)
    hg = jnp.dot(xb, wsg_ref[...], preferred_element_type=F32)
    hu = jnp.dot(xb, wsu_ref[...], preferred_element_type=F32)
    f = jnp.dot((jax.nn.silu(hg) * hu).astype(BF16), wsd_ref[...], preferred_element_type=F32)

    def wait(r, c):
        for k in range(TOP_K):
            copy(pos_ref, slot, r, k).wait()
        return c

    lax.fori_loop(0, tm, wait, 0)
    w = w_ref[...]
    wk = [jnp.broadcast_to(w[:, k:k + 1], (tm, LANES)) for k in range(TOP_K)]
    los, his = [], []
    for c in range(TOKEN_ROWS):
        acc_lo = f[:, c * LANES:(c + 1) * LANES]
        acc_hi = f[:, half + c * LANES:half + (c + 1) * LANES]
        for k in range(TOP_K):
            lo, hi = _unpack_halves(buf_ref[slot, k, pl.ds(c, tm, stride=TOKEN_ROWS), :])
            acc_lo = acc_lo + wk[k] * lo
            acc_hi = acc_hi + wk[k] * hi
        los.append(acc_lo)
        his.append(acc_hi)
    f = jnp.concatenate(los + his, axis=1)
    o_ref[...] = _residual_ln(x_ref[...], f, gate_ref, g_ref, b_ref)


def _combine(pos_flat, w, xm, x, mods, layer, wsg, wsu, wsd, g, b, ys):
    t, d = x.shape
    fs = wsg.shape[1]
    tm = 128
    ng = tm // GROUP
    last = t // tm - 1
    return pl.pallas_call(
        _combine_kernel,
        out_shape=jax.ShapeDtypeStruct((t, d), F32),
        grid=(t // tm,),
        in_specs=[
            pl.BlockSpec((tm * TOP_K,), lambda i: (i,), memory_space=pltpu.SMEM),
            pl.BlockSpec((tm * TOP_K,), lambda i: (jnp.minimum(i + 1, last),), memory_space=pltpu.SMEM),
            pl.BlockSpec((tm, TOP_K), lambda i: (i, 0)),
            pl.BlockSpec((tm * TOKEN_ROWS, LANES), lambda i: (i, 0)),
            pl.BlockSpec((tm, d), lambda i: (i, 0)),
            pl.BlockSpec((1, ng, 1, d), lambda i: (layer, i, 0, 2)),
            pl.BlockSpec((d, fs), lambda i: (0, 0)),
            pl.BlockSpec((d, fs), lambda i: (0, 0)),
            pl.BlockSpec((fs, d), lambda i: (0, 0)),
            pl.BlockSpec((1, d), lambda i: (0, 0)),
            pl.BlockSpec((1, d), lambda i: (0, 0)),
            pl.BlockSpec(memory_space=pl.ANY),
        ],
        out_specs=pl.BlockSpec((tm, d), lambda i: (i, 0)),
        scratch_shapes=[pltpu.VMEM((2, TOP_K, tm * TOKEN_ROWS, LANES), U32), pltpu.SemaphoreType.DMA((2,))],
        compiler_params=_cp(48, ("arbitrary",)),
        name="combine",
    )(pos_flat, pos_flat, w, xm, x, mods, wsg, wsu, wsd, g, b, ys)


def _moe(x, mods, layer, wr, br, wg, wu, wd, wsg, wsu, wsd, g, b):
    t = x.shape[0]
    ne = wr.shape[1]
    tile = EXPERT_TILE
    max_tiles = (t * TOP_K) // tile + ne
    xm, cw, mk, cnt = _router(x, mods, layer, wr, br)
    cnt = cnt[0].astype(I32)
    tiles = (cnt + tile - 1) // tile
    ends = jnp.cumsum(tiles)
    start = (ends - tiles) * tile
    n_used = ends[-1:]
    tile_ids = jnp.minimum(jnp.arange(max_tiles, dtype=I32), n_used[0] - 1)
    tile_expert = jnp.sum((ends[None, :] <= tile_ids[:, None]).astype(I32), axis=1)
    pos, w = _plan(mk, cw, start.astype(F32)[None, :])
    pos_flat = pos.reshape(-1)
    xs = _dispatch(start + cnt, tiles * tile - cnt, pos_flat, xm, max_tiles * tile)
    ys = _gmm(tile_expert, n_used.astype(I32), xs, wg, wu, wd, layer, max_tiles)
    return _combine(pos_flat, w, xm, x, mods, layer, wsg, wsu, wsd, g, b, ys)


def _expand_mods(mod, n_prompt_groups, n_streams):
    depth, _, n = mod.shape
    p = jnp.broadcast_to(mod[:, 0:1], (depth, n_prompt_groups, n))
    return jnp.concatenate([p, mod[:, 1:1 + n_streams]], axis=1)[:, :, None, :]


def kernel(x_prompt, x_sample, c_prompt, c_sample, cache_k, cache_v, w_ada_mix, b_ada_mix, w_ada_ffn, b_ada_ffn, ln_mix_g, ln_mix_b, ln_ffn_g, ln_ffn_b, w_qkv, w_o, w_uv, b_uv, ln_v_g, ln_v_b, w_s, b_s, w_cm_out, w_router, b_router, w_gate, w_up, w_down, ws_gate, ws_up, ws_down):
    bp, sp, d = x_prompt.shape
    bs, t_new, _ = x_sample.shape
    assert bp == 1 and t_new == GROUP and sp % ATT_BLOCK == 0 and d == N_HEADS * HEAD_DIM
    n_prompt = bp * sp
    n_sample = bs * t_new
    width = w_uv.shape[2] // 2
    past = cache_k.shape[2]

    x = jnp.concatenate([x_prompt.reshape(n_prompt, d), x_sample.reshape(n_sample, d)], axis=0)

    rows = 8 * ((1 + bs + 7) // 8)
    c_all = jnp.zeros((rows, d), F32).at[0:1].set(c_prompt).at[1:1 + bs].set(c_sample)
    mods_mix = _expand_mods(_ada(c_all, w_ada_mix, b_ada_mix), n_prompt // GROUP, bs)
    mods_ffn = _expand_mods(_ada(c_all, w_ada_ffn, b_ada_ffn), n_prompt // GROUP, bs)

    cpos = jnp.arange(CM_CHUNK)
    cmask = (cpos[None, :] // CHUNK) <= (cpos[:, None] // CHUNK)
    pair = (cpos[None, :] // t_new) == (cpos[:, None] // t_new)
    fold = cpos % t_new
    smask = (fold[None, :] // CHUNK) <= (fold[:, None] // CHUNK)
    ws_prompt = jnp.where(cmask, w_s, 0.0)
    ws_sample = jnp.where(jnp.logical_and(pair, smask), w_s[:, :, fold][:, :, :, fold], 0.0)
    mix = jnp.stack([ws_prompt, ws_sample], axis=1).astype(BF16)
    gw = width // CM_GROUPS
    bias_p = jnp.repeat(jnp.swapaxes(b_s, 1, 2), gw, axis=2)
    mix_bias = jnp.stack([bias_p, bias_p[:, fold]], axis=1)

    ck = cache_k.reshape(cache_k.shape[0], bs, past * N_HEADS, HEAD_DIM)
    cv = cache_v.reshape(cache_v.shape[0], bs, past * N_HEADS, HEAD_DIM)

    ks, vs, cmv = [], [], []
    for i in range(DEPTH):
        j = i // 2
        if i % 2 == 0:
            qkv = _modmm(x, mods_mix, i, w_qkv[j].astype(BF16), jnp.zeros((1, 3 * d), F32), F32, False)
            o_p = _attn_prompt(qkv, n_prompt)
            o_s, cs = _attn_sample(qkv, ck, cv, j, n_prompt, bs, t_new)
            o_s = lax.cond(
                jnp.max(cs) > STICK_DONE,
                lambda: _attn_sample_rest(qkv, o_s, cs, ck, cv, j, n_prompt, bs, t_new),
                lambda: o_s)
            o = jnp.concatenate([o_p, o_s.astype(BF16)], axis=0)
            x = _proj_ln(o, x, mods_mix, i, w_o[j].astype(BF16), ln_mix_g[i][None], ln_mix_b[i][None])
            ks.append(qkv[:, d:2 * d])
            vs.append(qkv[:, 2 * d:])
        else:
            zz = _modmm(x, mods_mix, i, w_uv[j].astype(BF16), b_uv[j][None], BF16, True)
            x, v = _cm_mix(zz, x, mods_mix, i, mix[j], mix_bias[j], ln_v_g[j][None], ln_v_b[j][None],
                           w_cm_out[j].astype(BF16), ln_mix_g[i][None], ln_mix_b[i][None], n_prompt)
            cmv.append(v)
        x = _moe(x, mods_ffn, i, w_router[i], b_router[i][None], w_gate, w_up, w_down,
                 ws_gate[i].astype(BF16), ws_up[i].astype(BF16), ws_down[i].astype(BF16),
                 ln_ffn_g[i][None], ln_ffn_b[i][None])

    def heads(a, lo, hi, b):
        return jnp.stack([m[lo:hi].reshape(b, -1, N_HEADS, HEAD_DIM) for m in a])

    return (
        x[:n_prompt].reshape(bp, sp, d),
        x[n_prompt:].reshape(bs, t_new, d),
        heads(ks, 0, n_prompt, bp),
        heads(vs, 0, n_prompt, bp),
        heads(ks, n_prompt, n_prompt + n_sample, bs),
        heads(vs, n_prompt, n_prompt + n_sample, bs),
        jnp.stack([v[n_prompt - CM_CHUNK:n_prompt].reshape(bp, CM_CHUNK, width) for v in cmv]),
        jnp.stack([v[n_prompt:].reshape(bs, t_new, width) for v in cmv]),
    )
```

```python
import functools

import jax
import jax.numpy as jnp
from jax import lax
from jax.experimental import pallas as pl
from jax.experimental.pallas import tpu as pltpu

F32 = jnp.float32
BF16 = jnp.bfloat16
I32 = jnp.int32

DEPTH = 4
N_HEADS = 16
HEAD_DIM = 128
CHUNK = 64
CM_CHUNK = 128
CM_GROUPS = 16
N_EXPERTS = 64
N_EXPERT_GROUPS = 8
TOPK_GROUPS = 4
TOP_K = 8
ROUTED_SCALE = 2.5
ALPHA = (2 * DEPTH) ** 0.25
LN_EPS = 1e-5

GROUP = 64
MIB = 1024 * 1024
STICK_DONE = -104.0
ATT_BLOCK = 256
EXPERT_TILE = 256


def _cp(vmem_mib, sem):
    return pltpu.CompilerParams(dimension_semantics=sem, vmem_limit_bytes=vmem_mib * MIB)


def _ln(y, g, b):
    mu = jnp.mean(y, axis=-1, keepdims=True)
    yc = y - mu
    var = jnp.mean(yc * yc, axis=-1, keepdims=True)
    return yc * lax.rsqrt(var + LN_EPS) * g + b


def _modulate(x, sh_ref, sc_ref):
    tm, d = x.shape
    g = sh_ref.shape[1]
    xg = x.reshape(g, tm // g, d)
    return (xg * sc_ref[0] + sh_ref[0]).reshape(tm, d)


def _residual_ln(x, f, gate_ref, g_ref, b_ref):
    tm, d = x.shape
    g = gate_ref.shape[1]
    y = ALPHA * x + (gate_ref[0] * f.reshape(g, tm // g, d)).reshape(tm, d)
    return _ln(y, g_ref[...], b_ref[...])


TOKEN_ROWS = 8
LANES = 128
U32 = jnp.uint32


def _pack_halves(y):
    n = y.shape[1] // 2
    lo = lax.bitcast_convert_type(y[:, :n].astype(BF16).astype(F32), U32)
    hi = lax.bitcast_convert_type(y[:, n:].astype(BF16).astype(F32), U32)
    return hi | (lo >> 16)


def _unpack_halves(w):
    lo = lax.bitcast_convert_type(w << 16, F32)
    hi = lax.bitcast_convert_type(w & jnp.uint32(0xFFFF0000), F32)
    return lo, hi


def _store_token_tiles(ref, words):
    tm = words.shape[0]
    for c in range(TOKEN_ROWS):
        ref[pl.ds(c, tm, stride=TOKEN_ROWS), :] = words[:, c * LANES:(c + 1) * LANES]


def _load_token_tiles(ref):
    tm = ref.shape[0] // TOKEN_ROWS
    return jnp.concatenate([ref[pl.ds(c, tm, stride=TOKEN_ROWS), :] for c in range(TOKEN_ROWS)], axis=1)


def _token_tile(ref, i):
    return ref.at[pl.ds(pl.multiple_of(i * TOKEN_ROWS, TOKEN_ROWS), TOKEN_ROWS)]


def _ada_kernel(c_ref, w_ref, b_ref, o_ref, *, d_model, tn):
    j = pl.program_id(1)
    a = jax.nn.silu(c_ref[...]).astype(BF16)
    acc = jnp.dot(a, w_ref[0].astype(BF16), preferred_element_type=F32) + b_ref[0]
    o_ref[0] = acc + jnp.where(j * tn >= d_model, 1.0, 0.0).astype(F32)


def _ada(c_all, w, b):
    depth, d, n = w.shape
    r = c_all.shape[0]
    tn = 1024
    return pl.pallas_call(
        functools.partial(_ada_kernel, d_model=d, tn=tn),
        out_shape=jax.ShapeDtypeStruct((depth, r, n), F32),
        grid=(depth, n // tn),
        in_specs=[
            pl.BlockSpec((r, d), lambda l, j: (0, 0)),
            pl.BlockSpec((1, d, tn), lambda l, j: (l, 0, j)),
            pl.BlockSpec((1, 1, tn), lambda l, j: (l, 0, j)),
        ],
        out_specs=pl.BlockSpec((1, r, tn), lambda l, j: (l, 0, j)),
        compiler_params=_cp(40, ("parallel", "parallel")),
        name="ada",
    )(c_all, w, b.reshape(depth, 1, n))


def _modmm_kernel(x_ref, sh_ref, sc_ref, w_ref, b_ref, o_ref, *, gelu):
    h = _modulate(x_ref[...], sh_ref, sc_ref).astype(BF16)
    acc = jnp.dot(h, w_ref[...], preferred_element_type=F32) + b_ref[...]
    if gelu:
        acc = jax.nn.gelu(acc)
    o_ref[...] = acc.astype(o_ref.dtype)


def _modmm(x, mods, layer, w, b, out_dtype, gelu):
    t, d = x.shape
    n = w.shape[1]
    tm, tn = 512, 2048
    g = tm // GROUP
    return pl.pallas_call(
        functools.partial(_modmm_kernel, gelu=gelu),
        out_shape=jax.ShapeDtypeStruct((t, n), out_dtype),
        grid=(n // tn, t // tm),
        in_specs=[
            pl.BlockSpec((tm, d), lambda j, i: (i, 0)),
            pl.BlockSpec((1, g, 1, d), lambda j, i: (layer, i, 0, 0)),
            pl.BlockSpec((1, g, 1, d), lambda j, i: (layer, i, 0, 1)),
            pl.BlockSpec((d, tn), lambda j, i: (0, j)),
            pl.BlockSpec((1, tn), lambda j, i: (0, j)),
        ],
        out_specs=pl.BlockSpec((tm, tn), lambda j, i: (i, j)),
        compiler_params=_cp(48, ("parallel", "parallel")),
        name="modmm",
    )(x, mods, mods, w, b)


def _qkv_kernel(x_ref, sh_ref, sc_ref, w_ref, *rest):
    qkv_ref, kf_ref, vf_ref = rest[-3:]
    h = _modulate(x_ref[...], sh_ref, sc_ref).astype(BF16)
    acc = jnp.dot(h, w_ref[...], preferred_element_type=F32)
    qkv_ref[...] = acc.astype(BF16)
    tm = acc.shape[0]
    d = acc.shape[1] // 3
    for hd in range(N_HEADS):
        cols = slice(hd * HEAD_DIM, (hd + 1) * HEAD_DIM)
        kf_ref[0, pl.ds(hd, tm, stride=N_HEADS), :] = acc[:, d:2 * d][:, cols]
        vf_ref[0, pl.ds(hd, tm, stride=N_HEADS), :] = acc[:, 2 * d:][:, cols]


def _qkv(x, mods, layer, w, row0, n_rows, slot, n_slots, kf_prev, vf_prev):
    d = x.shape[1]
    tm = 256
    g = tm // GROUP
    off = row0 // tm
    flat = (n_slots, n_rows * N_HEADS, HEAD_DIM)
    in_specs = [
        pl.BlockSpec((tm, d), lambda i: (i + off, 0)),
        pl.BlockSpec((1, g, 1, d), lambda i: (layer, i + off, 0, 0)),
        pl.BlockSpec((1, g, 1, d), lambda i: (layer, i + off, 0, 1)),
        pl.BlockSpec((d, 3 * d), lambda i: (0, 0), pipeline_mode=pl.Buffered(1)),
    ]
    args = [x, mods, mods, w]
    aliases = {}
    if kf_prev is not None:
        in_specs += [pl.BlockSpec(memory_space=pl.ANY), pl.BlockSpec(memory_space=pl.ANY)]
        args += [kf_prev, vf_prev]
        aliases = {4: 1, 5: 2}
    return pl.pallas_call(
        _qkv_kernel,
        out_shape=(jax.ShapeDtypeStruct((n_rows, 3 * d), BF16),
                   jax.ShapeDtypeStruct(flat, F32), jax.ShapeDtypeStruct(flat, F32)),
        grid=(n_rows // tm,),
        in_specs=in_specs,
        out_specs=(pl.BlockSpec((tm, 3 * d), lambda i: (i, 0)),
                   pl.BlockSpec((1, tm * N_HEADS, HEAD_DIM), lambda i: (slot, i, 0)),
                   pl.BlockSpec((1, tm * N_HEADS, HEAD_DIM), lambda i: (slot, i, 0))),
        input_output_aliases=aliases,
        compiler_params=_cp(56, ("parallel",)),
        name="qkv",
    )(*args)


def _sb_block(q, k, v, csum, u, mask):
    z = lax.dot_general(q, k, (((1,), (1,)), ((), ())), preferred_element_type=F32)
    z = z * (HEAD_DIM ** -0.5)
    t = jnp.log(1.0 + jnp.exp(-jnp.abs(z)))
    log_beta = -(jnp.maximum(-z, 0.0) + t)
    l1 = -(jnp.maximum(z, 0.0) + t)
    if mask is not None:
        l1 = jnp.where(mask, l1, 0.0)
    hi = l1.astype(BF16)
    lo = (l1 - hi.astype(F32)).astype(BF16)
    s = jnp.dot(hi, u, preferred_element_type=F32) + jnp.dot(lo, u, preferred_element_type=F32)
    a = jnp.exp(log_beta + s + csum)
    if mask is not None:
        a = jnp.where(mask, a, 0.0)
    o = jnp.dot(a.astype(BF16), v, preferred_element_type=F32)
    return o, csum + s[:, :1] + l1[:, :1]


def _causal_mask(n):
    row = lax.broadcasted_iota(I32, (n, n), 0)
    col = lax.broadcasted_iota(I32, (n, n), 1)
    return col < row


def _attn_prompt_kernel(q_ref, k_ref, v_ref, u_ref, o_ref):
    i = pl.program_id(1)
    bq = q_ref.shape[0]
    heads = q_ref.shape[1] // HEAD_DIM
    u = u_ref[...]
    mask = _causal_mask(bq)

    def kv(b, cols):
        start = pl.multiple_of(b * bq, bq)
        return (k_ref[pl.ds(start, bq), cols].astype(BF16), v_ref[pl.ds(start, bq), cols].astype(BF16))

    prev = jnp.maximum(i - 1, 0)
    has_prev = i > 0
    qs, os, css = [], [], []
    for h in range(heads):
        cols = slice(h * HEAD_DIM, (h + 1) * HEAD_DIM)
        q = q_ref[:, cols].astype(BF16)
        k0, v0 = kv(i, cols)
        o, cs = _sb_block(q, k0, v0, jnp.zeros((bq, 1), F32), u, mask)
        k1, v1 = kv(prev, cols)
        do, cs1 = _sb_block(q, k1, v1, cs, u, None)
        qs.append(q)
        os.append(o + jnp.where(has_prev, do, 0.0))
        css.append(jnp.where(has_prev, cs1, cs))

    def cond(c):
        b, _, css = c
        live = jnp.max(css[0])
        for cs in css[1:]:
            live = jnp.maximum(live, jnp.max(cs))
        return jnp.logical_and(b >= 0, live > STICK_DONE)

    def body(c):
        b, os, css = c
        new_os, new_css = [], []
        for h in range(heads):
            cols = slice(h * HEAD_DIM, (h + 1) * HEAD_DIM)
            kb, vb = kv(b, cols)
            do, cs = _sb_block(qs[h], kb, vb, css[h], u, None)
            new_os.append(os[h] + do)
            new_css.append(cs)
        return b - 1, tuple(new_os), tuple(new_css)

    _, os, _ = lax.while_loop(cond, body, (i - 2, tuple(os), tuple(css)))
    for h in range(heads):
        o_ref[:, h * HEAD_DIM:(h + 1) * HEAD_DIM] = os[h].astype(o_ref.dtype)


def _strict_lower(n):
    r = jnp.arange(n)
    return (r[:, None] > r[None, :]).astype(BF16)


def _attn_prompt(qkv, n_prompt):
    d = N_HEADS * HEAD_DIM
    bq = ATT_BLOCK
    hp = 4
    groups = N_HEADS // hp
    wcols = hp * HEAD_DIM
    return pl.pallas_call(
        _attn_prompt_kernel,
        out_shape=jax.ShapeDtypeStruct((n_prompt, d), BF16),
        grid=(groups, n_prompt // bq),
        in_specs=[
            pl.BlockSpec((bq, wcols), lambda h, i: (i, h)),
            pl.BlockSpec((n_prompt, wcols), lambda h, i: (0, groups + h)),
            pl.BlockSpec((n_prompt, wcols), lambda h, i: (0, 2 * groups + h)),
            pl.BlockSpec((bq, bq), lambda h, i: (0, 0)),
        ],
        out_specs=pl.BlockSpec((bq, wcols), lambda h, i: (i, h)),
        compiler_params=_cp(48, ("parallel", "parallel")),
        name="attn_prompt",
    )(qkv, qkv, qkv, _strict_lower(bq))


def _attn_sample_kernel(q_ref, kn_ref, vn_ref, ck_ref, cv_ref, un_ref, uc_ref, o_ref, cs_ref):
    tq = q_ref.shape[0]
    bk = uc_ref.shape[0]
    mask = _causal_mask(tq)
    for h in range(N_HEADS):
        cols = slice(h * HEAD_DIM, (h + 1) * HEAD_DIM)
        q = q_ref[:, cols].astype(BF16)
        o, cs = _sb_block(q, kn_ref[:, cols].astype(BF16), vn_ref[:, cols].astype(BF16),
                          jnp.zeros((tq, 1), F32), un_ref[...], mask)
        kc = ck_ref[0, 0, pl.ds(h, bk, stride=N_HEADS), :].astype(BF16)
        vc = cv_ref[0, 0, pl.ds(h, bk, stride=N_HEADS), :].astype(BF16)
        do, cs = _sb_block(q, kc, vc, cs, uc_ref[...], None)
        o_ref[:, cols] = o + do
        cs_ref[:, cols] = jnp.broadcast_to(cs, (tq, HEAD_DIM))


def _attn_sample(qkv, cache_k, cache_v, layer, n_prompt, n_streams, t_new):
    d = N_HEADS * HEAD_DIM
    past = cache_k.shape[2] // N_HEADS
    bk = ATT_BLOCK
    pb = n_prompt // t_new
    last = past // bk - 1
    return pl.pallas_call(
        _attn_sample_kernel,
        out_shape=(jax.ShapeDtypeStruct((n_streams * t_new, d), F32),
                   jax.ShapeDtypeStruct((n_streams * t_new, d), F32)),
        grid=(n_streams,),
        in_specs=[
            pl.BlockSpec((t_new, d), lambda s: (pb + s, 0)),
            pl.BlockSpec((t_new, d), lambda s: (pb + s, 1)),
            pl.BlockSpec((t_new, d), lambda s: (pb + s, 2)),
            pl.BlockSpec((1, 1, bk * N_HEADS, HEAD_DIM), lambda s: (layer, s, last, 0)),
            pl.BlockSpec((1, 1, bk * N_HEADS, HEAD_DIM), lambda s: (layer, s, last, 0)),
            pl.BlockSpec((t_new, t_new), lambda s: (0, 0)),
            pl.BlockSpec((bk, bk), lambda s: (0, 0)),
        ],
        out_specs=(pl.BlockSpec((t_new, d), lambda s: (s, 0)),
                   pl.BlockSpec((t_new, d), lambda s: (s, 0))),
        compiler_params=_cp(32, ("parallel",)),
        name="attn_sample",
    )(qkv, qkv, qkv, cache_k, cache_v, _strict_lower(t_new), _strict_lower(bk))


def _attn_sample_rest_kernel(q_ref, oin_ref, csin_ref, ck_ref, cv_ref, u_ref, o_ref, cs_ref):
    b = pl.program_id(1)
    bk = u_ref.shape[0]

    @pl.when(b == 0)
    def _():
        o_ref[...] = oin_ref[...]
        cs_ref[...] = csin_ref[...]

    for h in range(N_HEADS):
        cols = slice(h * HEAD_DIM, (h + 1) * HEAD_DIM)
        cs = cs_ref[:, cols][:, :1]

        @pl.when(jnp.max(cs) > STICK_DONE)
        def _(h=h, cols=cols, cs=cs):
            q = q_ref[:, cols].astype(BF16)
            kc = ck_ref[0, 0, pl.ds(h, bk, stride=N_HEADS), :].astype(BF16)
            vc = cv_ref[0, 0, pl.ds(h, bk, stride=N_HEADS), :].astype(BF16)
            do, cs2 = _sb_block(q, kc, vc, cs, u_ref[...], None)
            o_ref[:, cols] += do
            cs_ref[:, cols] = jnp.broadcast_to(cs2, (cs2.shape[0], HEAD_DIM))


def _attn_sample_rest(qkv, o_part, cs_part, cache_k, cache_v, layer, n_prompt, n_streams, t_new):
    d = N_HEADS * HEAD_DIM
    past = cache_k.shape[2] // N_HEADS
    bk = ATT_BLOCK
    pb = n_prompt // t_new
    nb = past // bk - 1
    o, _ = pl.pallas_call(
        _attn_sample_rest_kernel,
        out_shape=(jax.ShapeDtypeStruct((n_streams * t_new, d), F32),
                   jax.ShapeDtypeStruct((n_streams * t_new, d), F32)),
        grid=(n_streams, nb),
        in_specs=[
            pl.BlockSpec((t_new, d), lambda s, b: (pb + s, 0)),
            pl.BlockSpec((t_new, d), lambda s, b: (s, 0)),
            pl.BlockSpec((t_new, d), lambda s, b: (s, 0)),
            pl.BlockSpec((1, 1, bk * N_HEADS, HEAD_DIM), lambda s, b: (layer, s, nb - 1 - b, 0)),
            pl.BlockSpec((1, 1, bk * N_HEADS, HEAD_DIM), lambda s, b: (layer, s, nb - 1 - b, 0)),
            pl.BlockSpec((bk, bk), lambda s, b: (0, 0)),
        ],
        out_specs=(pl.BlockSpec((t_new, d), lambda s, b: (s, 0)),
                   pl.BlockSpec((t_new, d), lambda s, b: (s, 0))),
        compiler_params=_cp(32, ("parallel", "arbitrary")),
        name="attn_sample_rest",
    )(qkv, o_part, cs_part, cache_k, cache_v, _strict_lower(bk))
    return o


def _proj_ln_kernel(ap_ref, as_ref, x_ref, gate_ref, w_ref, g_ref, b_ref, o_ref, *, prompt_tiles):
    a = jnp.where(pl.program_id(0) < prompt_tiles, ap_ref[...], as_ref[...].astype(BF16))
    f = jnp.dot(a, w_ref[...], preferred_element_type=F32)
    o_ref[...] = _residual_ln(x_ref[...], f, gate_ref, g_ref, b_ref)


def _proj_ln(a_prompt, a_sample, x, mods, layer, w, g, b):
    t, d = x.shape
    kdim = a_prompt.shape[1]
    tm = 256
    ng = tm // GROUP
    pt = a_prompt.shape[0] // tm
    return pl.pallas_call(
        functools.partial(_proj_ln_kernel, prompt_tiles=pt),
        out_shape=jax.ShapeDtypeStruct((t, d), F32),
        grid=(t // tm,),
        in_specs=[
            pl.BlockSpec((tm, kdim), lambda i: (jnp.minimum(i, pt - 1), 0)),
            pl.BlockSpec((tm, kdim), lambda i: (jnp.maximum(i - pt, 0), 0)),
            pl.BlockSpec((tm, d), lambda i: (i, 0)),
            pl.BlockSpec((1, ng, 1, d), lambda i: (layer, i, 0, 2)),
            pl.BlockSpec((kdim, d), lambda i: (0, 0)),
            pl.BlockSpec((1, d), lambda i: (0, 0)),
            pl.BlockSpec((1, d), lambda i: (0, 0)),
        ],
        out_specs=pl.BlockSpec((tm, d), lambda i: (i, 0)),
        compiler_params=_cp(40, ("parallel",)),
        name="proj_ln",
    )(a_prompt, a_sample, x, mods, w, g, b)


def _cm_mix_kernel(u_ref, vraw_ref, x_ref, gate_ref, mx_ref, bias_ref, lvg_ref, lvb_ref,
                   wo_ref, g_ref, b_ref, o_ref, v_ref, gated_ref):
    tm, width = u_ref.shape
    gw = width // CM_GROUPS
    v = _ln(vraw_ref[...].astype(F32), lvg_ref[...], lvb_ref[...])
    v_ref[...] = v
    vb = v.astype(BF16)
    for c in range(tm // CM_CHUNK):
        rows = slice(c * CM_CHUNK, (c + 1) * CM_CHUNK)
        for g in range(CM_GROUPS):
            cols = slice(g * gw, (g + 1) * gw)
            mixed = jnp.dot(mx_ref[0, g], vb[rows, cols], preferred_element_type=F32)
            mixed = mixed + bias_ref[0, :, cols]
            gated_ref[rows, cols] = (u_ref[rows, cols].astype(F32) * mixed).astype(BF16)
    f = jnp.dot(gated_ref[...], wo_ref[...], preferred_element_type=F32)
    o_ref[...] = _residual_ln(x_ref[...], f, gate_ref, g_ref, b_ref)


def _cm_mix(zz, x, mods, layer, mx, bias, lvg, lvb, wo, g, b, n_prompt):
    t, d = x.shape
    width = zz.shape[1] // 2
    tm = CM_CHUNK
    ng = tm // GROUP
    n_prompt_tiles = n_prompt // tm
    kind = lambda i: jnp.where(i < n_prompt_tiles, 0, 1)
    return pl.pallas_call(
        _cm_mix_kernel,
        out_shape=(jax.ShapeDtypeStruct((t, d), F32), jax.ShapeDtypeStruct((t, width), F32)),
        grid=(t // tm,),
        in_specs=[
            pl.BlockSpec((tm, width), lambda i: (i, 0)),
            pl.BlockSpec((tm, width), lambda i: (i, 1)),
            pl.BlockSpec((tm, d), lambda i: (i, 0)),
            pl.BlockSpec((1, ng, 1, d), lambda i: (layer, i, 0, 2)),
            pl.BlockSpec((1, CM_GROUPS, CM_CHUNK, CM_CHUNK), lambda i: (kind(i), 0, 0, 0)),
            pl.BlockSpec((1, CM_CHUNK, width), lambda i: (kind(i), 0, 0)),
            pl.BlockSpec((1, width), lambda i: (0, 0)),
            pl.BlockSpec((1, width), lambda i: (0, 0)),
            pl.BlockSpec((width, d), lambda i: (0, 0), pipeline_mode=pl.Buffered(1)),
            pl.BlockSpec((1, d), lambda i: (0, 0)),
            pl.BlockSpec((1, d), lambda i: (0, 0)),
        ],
        out_specs=(pl.BlockSpec((tm, d), lambda i: (i, 0)),
                   pl.BlockSpec((tm, width), lambda i: (i, 0))),
        scratch_shapes=[pltpu.VMEM((tm, width), BF16)],
        compiler_params=_cp(48, ("parallel",)),
        name="cm_mix",
    )(zz, zz, x, mods, mx, bias, lvg, lvb, wo, g, b)


def _split_dot(x, w):
    xh = x.astype(BF16)
    xl = (x - xh.astype(F32)).astype(BF16)
    wh = w.astype(BF16)
    wl = (w - wh.astype(F32)).astype(BF16)
    d = lambda a, b: jnp.dot(a, b, preferred_element_type=F32)
    return d(xh, wh) + (d(xh, wl) + d(xl, wh))


def _first_argmax(m, lane, n):
    m1 = jnp.max(m, axis=1, keepdims=True)
    i1 = jnp.min(jnp.where(m == m1, lane, n), axis=1, keepdims=True)
    return m1, lane == i1


def _router_kernel(x_ref, sh_ref, sc_ref, wr_ref, br_ref, xm_ref, cw_ref, mk_ref, cnt_ref):
    i = pl.program_id(0)
    xm = _modulate(x_ref[...], sh_ref, sc_ref)
    _store_token_tiles(xm_ref, _pack_halves(xm))
    scores = jax.nn.sigmoid(_split_dot(xm, wr_ref[...]))
    sel = scores + br_ref[...]
    tm, ne = sel.shape
    per_group = ne // N_EXPERT_GROUPS
    lane = lax.broadcasted_iota(I32, (tm, ne), 1)
    grp = lane // per_group
    neg = -jnp.inf
    gs = []
    for g in range(N_EXPERT_GROUPS):
        m = jnp.where(grp == g, sel, neg)
        m1, pick = _first_argmax(m, lane, ne)
        m2 = jnp.max(jnp.where(pick, neg, m), axis=1, keepdims=True)
        gs.append(m1 + m2)
    allowed = jnp.zeros((tm, ne), F32)
    for g in range(N_EXPERT_GROUPS):
        rank = jnp.zeros((tm, 1), I32)
        for g2 in range(N_EXPERT_GROUPS):
            if g2 == g:
                continue
            ahead = gs[g2] > gs[g]
            if g2 < g:
                ahead = jnp.logical_or(ahead, gs[g2] == gs[g])
            rank = rank + ahead.astype(I32)
        allowed = allowed + jnp.where(grp == g, (rank < TOPK_GROUPS).astype(F32), 0.0)
    ms = jnp.where(allowed > 0.0, sel, neg)
    chosen = jnp.zeros((tm, ne), jnp.bool_)
    for _ in range(TOP_K):
        _, pick = _first_argmax(ms, lane, ne)
        chosen = jnp.logical_or(chosen, pick)
        ms = jnp.where(pick, neg, ms)
    tw = jnp.where(chosen, scores, 0.0)
    cw_ref[...] = tw / jnp.sum(tw, axis=1, keepdims=True) * ROUTED_SCALE
    mk_ref[...] = chosen.astype(BF16)

    @pl.when(i == 0)
    def _():
        cnt_ref[...] = jnp.zeros_like(cnt_ref)

    cnt_ref[...] += jnp.sum(chosen.astype(F32), axis=0, keepdims=True)


def _router(x, mods, layer, wr, br):
    t, d = x.shape
    ne = wr.shape[1]
    tm = 256
    ng = tm // GROUP
    return pl.pallas_call(
        _router_kernel,
        out_shape=(jax.ShapeDtypeStruct((t * TOKEN_ROWS, LANES), U32), jax.ShapeDtypeStruct((t, ne), F32),
                   jax.ShapeDtypeStruct((t, ne), BF16), jax.ShapeDtypeStruct((1, ne), F32)),
        grid=(t // tm,),
        in_specs=[
            pl.BlockSpec((tm, d), lambda i: (i, 0)),
            pl.BlockSpec((1, ng, 1, d), lambda i: (layer, i, 0, 0)),
            pl.BlockSpec((1, ng, 1, d), lambda i: (layer, i, 0, 1)),
            pl.BlockSpec((d, ne), lambda i: (0, 0)),
            pl.BlockSpec((1, ne), lambda i: (0, 0)),
        ],
        out_specs=(pl.BlockSpec((tm * TOKEN_ROWS, LANES), lambda i: (i, 0)),
                   pl.BlockSpec((tm, ne), lambda i: (i, 0)),
                   pl.BlockSpec((tm, ne), lambda i: (i, 0)), pl.BlockSpec((1, ne), lambda i: (0, 0))),
        compiler_params=_cp(32, ("arbitrary",)),
        name="router",
    )(x, mods, mods, wr, br)


def _plan_kernel(mk_ref, cw_ref, start_ref, ltri_ref, upper_ref, pos_ref, w_ref, carry_ref):
    i = pl.program_id(0)

    @pl.when(i == 0)
    def _():
        carry_ref[...] = jnp.zeros_like(carry_ref)

    mk = mk_ref[...]
    rank = jnp.dot(ltri_ref[...], mk, preferred_element_type=F32)
    pos = start_ref[...] + carry_ref[...] + rank
    order = jnp.dot(mk, upper_ref[...], preferred_element_type=F32)
    chosen = mk > 0
    cw = cw_ref[...]
    for k in range(TOP_K):
        pick = jnp.logical_and(chosen, order == k)
        pos_ref[:, k:k + 1] = jnp.sum(jnp.where(pick, pos, 0.0), axis=1, keepdims=True).astype(I32)
        w_ref[:, k:k + 1] = jnp.sum(jnp.where(pick, cw, 0.0), axis=1, keepdims=True)
    carry_ref[...] += jnp.sum(mk.astype(F32), axis=0, keepdims=True)


def _plan(mk, cw, start):
    t, ne = mk.shape
    tm = 256
    r = jnp.arange(ne)
    upper = (r[:, None] < r[None, :]).astype(BF16)
    return pl.pallas_call(
        _plan_kernel,
        out_shape=(jax.ShapeDtypeStruct((t, TOP_K), I32), jax.ShapeDtypeStruct((t, TOP_K), F32)),
        grid=(t // tm,),
        in_specs=[
            pl.BlockSpec((tm, ne), lambda i: (i, 0)),
            pl.BlockSpec((tm, ne), lambda i: (i, 0)),
            pl.BlockSpec((1, ne), lambda i: (0, 0)),
            pl.BlockSpec((tm, tm), lambda i: (0, 0)),
            pl.BlockSpec((ne, ne), lambda i: (0, 0)),
        ],
        out_specs=(pl.BlockSpec((tm, TOP_K), lambda i: (i, 0)), pl.BlockSpec((tm, TOP_K), lambda i: (i, 0))),
        scratch_shapes=[pltpu.VMEM((1, ne), F32)],
        compiler_params=_cp(32, ("arbitrary",)),
        name="plan",
    )(mk, cw, start, _strict_lower(tm), upper)


def _dispatch_kernel(fill_ref, pad_ref, pos_ref, xm_ref, xs_ref, zeros_ref, sem):
    i = pl.program_id(0)
    tm = xm_ref.shape[0] // TOKEN_ROWS
    ne = fill_ref.shape[0]
    pad_bits = (zeros_ref.shape[0] // TOKEN_ROWS).bit_length() - 1

    def row_copy(r, k):
        return pltpu.make_async_copy(_token_tile(xm_ref, r), _token_tile(xs_ref, pos_ref[r * TOP_K + k]), sem)

    def fill(e, wait):
        first, n = fill_ref[e], pad_ref[e]
        for bit in range(pad_bits):
            size = (1 << bit) * TOKEN_ROWS
            done = (n >> (bit + 1)) << (bit + 1)
            dst = pl.multiple_of((first + done) * TOKEN_ROWS, TOKEN_ROWS)
            piece = pltpu.make_async_copy(zeros_ref.at[pl.ds(0, size)], xs_ref.at[pl.ds(dst, size)], sem)

            @pl.when(((n >> bit) & 1) == 1)
            def _(piece=piece):
                if wait:
                    piece.wait()
                else:
                    piece.start()

    @pl.when(i == 0)
    def _():
        zeros_ref[...] = jnp.zeros_like(zeros_ref)

        def start(e, c):
            fill(e, False)
            return c

        def wait(e, c):
            fill(e, True)
            return c

        lax.fori_loop(0, ne, start, 0)
        lax.fori_loop(0, ne, wait, 0)

    def start(r, c):
        for k in range(TOP_K):
            row_copy(r, k).start(priority=k % 2)
        return c

    def wait(r, c):
        for k in range(TOP_K):
            row_copy(r, k).wait()
        return c

    lax.fori_loop(0, tm, start, 0)
    lax.fori_loop(0, tm, wait, 0)


def _dispatch(fill, pad, pos_flat, xm, n_slots):
    t = xm.shape[0] // TOKEN_ROWS
    tm = 256
    return pl.pallas_call(
        _dispatch_kernel,
        out_shape=jax.ShapeDtypeStruct((n_slots * TOKEN_ROWS, LANES), U32),
        grid_spec=pltpu.PrefetchScalarGridSpec(
            num_scalar_prefetch=2,
            grid=(t // tm,),
            in_specs=[
                pl.BlockSpec((tm * TOP_K,), lambda i, fill, pad: (i,), memory_space=pltpu.SMEM),
                pl.BlockSpec((tm * TOKEN_ROWS, LANES), lambda i, fill, pad: (i, 0)),
            ],
            out_specs=pl.BlockSpec(memory_space=pl.ANY),
            scratch_shapes=[pltpu.VMEM((EXPERT_TILE * TOKEN_ROWS, LANES), U32), pltpu.SemaphoreType.DMA(())],
        ),
        compiler_params=_cp(32, ("arbitrary",)),
        name="dispatch",
    )(fill, pad, pos_flat, xm)


def _gmm_kernel(te_ref, nt_ref, nxt_ref, slot_ref, xs_ref, wg_ref, wu_ref, wd_ref, ys_ref,
                fg_ref, fu_ref, fd_ref, wgb_ref, wub_ref, wdb_ref, sems, *, layer):
    i = pl.program_id(0)
    e = te_ref[i]
    s = slot_ref[i]
    prev = te_ref[jnp.maximum(i - 1, 0)]

    def fetch(expert, slot):
        return (pltpu.make_async_copy(wg_ref.at[layer, expert], fg_ref.at[slot], sems.at[slot]),
                pltpu.make_async_copy(wu_ref.at[layer, expert], fu_ref.at[slot], sems.at[slot]),
                pltpu.make_async_copy(wd_ref.at[layer, expert], fd_ref.at[slot], sems.at[slot]))

    @pl.when(i == 0)
    def _():
        for c in fetch(e, s):
            c.start()

    @pl.when(jnp.logical_or(i == 0, e != prev))
    def _():
        for c in fetch(e, s):
            c.wait()
        wgb_ref[...] = fg_ref[s].astype(BF16)
        wub_ref[...] = fu_ref[s].astype(BF16)
        wdb_ref[...] = fd_ref[s].astype(BF16)

        @pl.when(nxt_ref[i] >= 0)
        def _():
            for c in fetch(nxt_ref[i], 1 - s):
                c.start()

    @pl.when(i < nt_ref[0])
    def _():
        lo, hi = _unpack_halves(_load_token_tiles(xs_ref))
        x = jnp.concatenate([lo, hi], axis=1).astype(BF16)
        hg = jnp.dot(x, wgb_ref[...], preferred_element_type=F32)
        hu = jnp.dot(x, wub_ref[...], preferred_element_type=F32)
        hid = (jax.nn.silu(hg) * hu).astype(BF16)
        y = jnp.dot(hid, wdb_ref[...], preferred_element_type=F32)
        _store_token_tiles(ys_ref, _pack_halves(y))


def _gmm(tile_expert, n_tiles_used, tile_next, tile_slot, xs, wg, wu, wd, layer, max_tiles):
    d, f = wg.shape[2], wg.shape[3]
    tm = EXPERT_TILE
    row = lambda i, te, nt, nx, sl: (jnp.minimum(i, nt[0] - 1), 0)
    return pl.pallas_call(
        functools.partial(_gmm_kernel, layer=layer),
        out_shape=jax.ShapeDtypeStruct(xs.shape, U32),
        grid_spec=pltpu.PrefetchScalarGridSpec(
            num_scalar_prefetch=4,
            grid=(max_tiles,),
            in_specs=[
                pl.BlockSpec((tm * TOKEN_ROWS, LANES), row),
                pl.BlockSpec(memory_space=pl.ANY),
                pl.BlockSpec(memory_space=pl.ANY),
                pl.BlockSpec(memory_space=pl.ANY),
            ],
            out_specs=pl.BlockSpec((tm * TOKEN_ROWS, LANES), row),
            scratch_shapes=[pltpu.VMEM((2, d, f), F32), pltpu.VMEM((2, d, f), F32), pltpu.VMEM((2, f, d), F32),
                            pltpu.VMEM((d, f), BF16), pltpu.VMEM((d, f), BF16), pltpu.VMEM((f, d), BF16),
                            pltpu.SemaphoreType.DMA((2,))],
        ),
        compiler_params=_cp(56, ("arbitrary",)),
        name="gmm",
    )(tile_expert, n_tiles_used, tile_next, tile_slot, xs, wg, wu, wd)


def _combine_kernel(pos_ref, nxt_ref, w_ref, xm_ref, x_ref, gate_ref, wsg_ref, wsu_ref, wsd_ref,
                    g_ref, b_ref, ys_ref, o_ref, buf_ref, sems):
    i = pl.program_id(0)
    n = pl.num_programs(0)
    tm, d = x_ref.shape
    half = d // 2
    slot = i % 2

    def copy(p_ref, s, r, k):
        return pltpu.make_async_copy(_token_tile(ys_ref, p_ref[r * TOP_K + k]),
                                     _token_tile(buf_ref.at[s, k], r), sems.at[s])

    def gather(p_ref, s):
        def start(r, c):
            for k in range(TOP_K):
                copy(p_ref, s, r, k).start(priority=k % 2)
            return c
        lax.fori_loop(0, tm, start, 0)

    @pl.when(i == 0)
    def _():
        gather(pos_ref, slot)

    @pl.when(i + 1 < n)
    def _():
        gather(nxt_ref, 1 - slot)

    lo, hi = _unpack_halves(_load_token_tiles(xm_ref))
    xb = jnp.concatenate([lo, hi], axis=1).astype(BF16)
    hg = jnp.dot(xb, wsg_ref[...], preferred_element_type=F32)
    hu = jnp.dot(xb, wsu_ref[...], preferred_element_type=F32)
    f = jnp.dot((jax.nn.silu(hg) * hu).astype(BF16), wsd_ref[...], preferred_element_type=F32)

    def wait(r, c):
        for k in range(TOP_K):
            copy(pos_ref, slot, r, k).wait()
        return c

    lax.fori_loop(0, tm, wait, 0)
    w = w_ref[...]
    wk = [jnp.broadcast_to(w[:, k:k + 1], (tm, LANES)) for k in range(TOP_K)]
    los, his = [], []
    for c in range(TOKEN_ROWS):
        acc_lo = f[:, c * LANES:(c + 1) * LANES]
        acc_hi = f[:, half + c * LANES:half + (c + 1) * LANES]
        for k in range(TOP_K):
            lo, hi = _unpack_halves(buf_ref[slot, k, pl.ds(c, tm, stride=TOKEN_ROWS), :])
            acc_lo = acc_lo + wk[k] * lo
            acc_hi = acc_hi + wk[k] * hi
        los.append(acc_lo)
        his.append(acc_hi)
    f = jnp.concatenate(los + his, axis=1)
    o_ref[...] = _residual_ln(x_ref[...], f, gate_ref, g_ref, b_ref)


def _combine(pos_flat, w, xm, x, mods, layer, wsg, wsu, wsd, g, b, ys):
    t, d = x.shape
    fs = wsg.shape[1]
    tm = 128
    ng = tm // GROUP
    last = t // tm - 1
    return pl.pallas_call(
        _combine_kernel,
        out_shape=jax.ShapeDtypeStruct((t, d), F32),
        grid=(t // tm,),
        in_specs=[
            pl.BlockSpec((tm * TOP_K,), lambda i: (i,), memory_space=pltpu.SMEM),
            pl.BlockSpec((tm * TOP_K,), lambda i: (jnp.minimum(i + 1, last),), memory_space=pltpu.SMEM),
            pl.BlockSpec((tm, TOP_K), lambda i: (i, 0)),
            pl.BlockSpec((tm * TOKEN_ROWS, LANES), lambda i: (i, 0)),
            pl.BlockSpec((tm, d), lambda i: (i, 0)),
            pl.BlockSpec((1, ng, 1, d), lambda i: (layer, i, 0, 2)),
            pl.BlockSpec((d, fs), lambda i: (0, 0)),
            pl.BlockSpec((d, fs), lambda i: (0, 0)),
            pl.BlockSpec((fs, d), lambda i: (0, 0)),
            pl.BlockSpec((1, d), lambda i: (0, 0)),
            pl.BlockSpec((1, d), lambda i: (0, 0)),
            pl.BlockSpec(memory_space=pl.ANY),
        ],
        out_specs=pl.BlockSpec((tm, d), lambda i: (i, 0)),
        scratch_shapes=[pltpu.VMEM((2, TOP_K, tm * TOKEN_ROWS, LANES), U32), pltpu.SemaphoreType.DMA((2,))],
        compiler_params=_cp(48, ("arbitrary",)),
        name="combine",
    )(pos_flat, pos_flat, w, xm, x, mods, wsg, wsu, wsd, g, b, ys)


def _moe(x, mods, layer, wr, br, wg, wu, wd, wsg, wsu, wsd, g, b):
    t = x.shape[0]
    ne = wr.shape[1]
    tile = EXPERT_TILE
    max_tiles = (t * TOP_K) // tile + ne
    xm, cw, mk, cnt = _router(x, mods, layer, wr, br)
    cnt = cnt[0].astype(I32)
    tiles = (cnt + tile - 1) // tile
    ends = jnp.cumsum(tiles)
    start = (ends - tiles) * tile
    n_used = ends[-1:]
    tile_ids = jnp.minimum(jnp.arange(max_tiles, dtype=I32), n_used[0] - 1)
    tile_expert = jnp.sum((ends[None, :] <= tile_ids[:, None]).astype(I32), axis=1)
    ids = jnp.arange(ne, dtype=I32)
    used = tiles > 0
    later_used = jnp.logical_and(ids[None, :] > ids[:, None], used[None, :])
    next_used = jnp.min(jnp.where(later_used, ids[None, :], ne), axis=1)
    next_used = jnp.where(next_used < ne, next_used, -1).astype(I32)
    slot = ((jnp.cumsum(used.astype(I32)) - 1) % 2).astype(I32)
    pos, w = _plan(mk, cw, start.astype(F32)[None, :])
    pos_flat = pos.reshape(-1)
    xs = _dispatch(start + cnt, tiles * tile - cnt, pos_flat, xm, max_tiles * tile)
    ys = _gmm(tile_expert, n_used.astype(I32), next_used[tile_expert], slot[tile_expert],
              xs, wg, wu, wd, layer, max_tiles)
    return _combine(pos_flat, w, xm, x, mods, layer, wsg, wsu, wsd, g, b, ys)


def _expand_mods(mod, n_prompt_groups, n_streams):
    depth, _, n = mod.shape
    p = jnp.broadcast_to(mod[:, 0:1], (depth, n_prompt_groups, n))
    return jnp.concatenate([p, mod[:, 1:1 + n_streams]], axis=1)[:, :, None, :]


def kernel(x_prompt, x_sample, c_prompt, c_sample, cache_k, cache_v, w_ada_mix, b_ada_mix, w_ada_ffn, b_ada_ffn, ln_mix_g, ln_mix_b, ln_ffn_g, ln_ffn_b, w_qkv, w_o, w_uv, b_uv, ln_v_g, ln_v_b, w_s, b_s, w_cm_out, w_router, b_router, w_gate, w_up, w_down, ws_gate, ws_up, ws_down):
    bp, sp, d = x_prompt.shape
    bs, t_new, _ = x_sample.shape
    assert bp == 1 and t_new == GROUP and sp % ATT_BLOCK == 0 and d == N_HEADS * HEAD_DIM
    n_prompt = bp * sp
    n_sample = bs * t_new
    width = w_uv.shape[2] // 2
    past = cache_k.shape[2]

    x = jnp.concatenate([x_prompt.reshape(n_prompt, d), x_sample.reshape(n_sample, d)], axis=0)

    rows = 8 * ((1 + bs + 7) // 8)
    c_all = jnp.zeros((rows, d), F32).at[0:1].set(c_prompt).at[1:1 + bs].set(c_sample)
    mods_mix = _expand_mods(_ada(c_all, w_ada_mix, b_ada_mix), n_prompt // GROUP, bs)
    mods_ffn = _expand_mods(_ada(c_all, w_ada_ffn, b_ada_ffn), n_prompt // GROUP, bs)

    cpos = jnp.arange(CM_CHUNK)
    cmask = (cpos[None, :] // CHUNK) <= (cpos[:, None] // CHUNK)
    pair = (cpos[None, :] // t_new) == (cpos[:, None] // t_new)
    fold = cpos % t_new
    smask = (fold[None, :] // CHUNK) <= (fold[:, None] // CHUNK)
    ws_prompt = jnp.where(cmask, w_s, 0.0)
    ws_sample = jnp.where(jnp.logical_and(pair, smask), w_s[:, :, fold][:, :, :, fold], 0.0)
    mix = jnp.stack([ws_prompt, ws_sample], axis=1).astype(BF16)
    gw = width // CM_GROUPS
    bias_p = jnp.repeat(jnp.swapaxes(b_s, 1, 2), gw, axis=2)
    mix_bias = jnp.stack([bias_p, bias_p[:, fold]], axis=1)

    ck = cache_k.reshape(cache_k.shape[0], bs, past * N_HEADS, HEAD_DIM)
    cv = cache_v.reshape(cache_v.shape[0], bs, past * N_HEADS, HEAD_DIM)

    n_sb = (DEPTH + 1) // 2
    kp = vp = ksm = vsm = None
    cmv = []
    for i in range(DEPTH):
        j = i // 2
        if i % 2 == 0:
            wq = w_qkv[j].astype(BF16)
            qkv_p, kp, vp = _qkv(x, mods_mix, i, wq, 0, n_prompt, j, n_sb, kp, vp)
            qkv_s, ksm, vsm = _qkv(x, mods_mix, i, wq, n_prompt, n_sample, j, n_sb, ksm, vsm)
            o_p = _attn_prompt(qkv_p, n_prompt)
            o_s, cs = _attn_sample(qkv_s, ck, cv, j, 0, bs, t_new)
            o_s = lax.cond(
                jnp.max(cs) > STICK_DONE,
                lambda: _attn_sample_rest(qkv_s, o_s, cs, ck, cv, j, 0, bs, t_new),
                lambda: o_s)
            x = _proj_ln(o_p, o_s, x, mods_mix, i, w_o[j].astype(BF16), ln_mix_g[i][None], ln_mix_b[i][None])
        else:
            zz = _modmm(x, mods_mix, i, w_uv[j].astype(BF16), b_uv[j][None], BF16, True)
            x, v = _cm_mix(zz, x, mods_mix, i, mix[j], mix_bias[j], ln_v_g[j][None], ln_v_b[j][None],
                           w_cm_out[j].astype(BF16), ln_mix_g[i][None], ln_mix_b[i][None], n_prompt)
            cmv.append(v)
        x = _moe(x, mods_ffn, i, w_router[i], b_router[i][None], w_gate, w_up, w_down,
                 ws_gate[i].astype(BF16), ws_up[i].astype(BF16), ws_down[i].astype(BF16),
                 ln_ffn_g[i][None], ln_ffn_b[i][None])

    return (
        x[:n_prompt].reshape(bp, sp, d),
        x[n_prompt:].reshape(bs, t_new, d),
        kp.reshape(n_sb, bp, sp, N_HEADS, HEAD_DIM),
        vp.reshape(n_sb, bp, sp, N_HEADS, HEAD_DIM),
        ksm.reshape(n_sb, bs, t_new, N_HEADS, HEAD_DIM),
        vsm.reshape(n_sb, bs, t_new, N_HEADS, HEAD_DIM),
        jnp.stack([v[n_prompt - CM_CHUNK:n_prompt].reshape(bp, CM_CHUNK, width) for v in cmv]),
        jnp.stack([v[n_prompt:].reshape(bs, t_new, width) for v in cmv]),
    )
```

```python
import functools

import jax
import jax.numpy as jnp
from jax import lax
from jax.experimental import pallas as pl
from jax.experimental.pallas import tpu as pltpu

F32 = jnp.float32
BF16 = jnp.bfloat16
I32 = jnp.int32

DEPTH = 4
N_HEADS = 16
HEAD_DIM = 128
CHUNK = 64
CM_CHUNK = 128
CM_GROUPS = 16
N_EXPERTS = 64
N_EXPERT_GROUPS = 8
TOPK_GROUPS = 4
TOP_K = 8
ROUTED_SCALE = 2.5
ALPHA = (2 * DEPTH) ** 0.25
LN_EPS = 1e-5

GROUP = 64
MIB = 1024 * 1024
STICK_DONE = -104.0
ATT_BLOCK = 256
EXPERT_TILE = 256


def _cp(vmem_mib, sem):
    return pltpu.CompilerParams(dimension_semantics=sem, vmem_limit_bytes=vmem_mib * MIB)


def _ln(y, g, b):
    mu = jnp.mean(y, axis=-1, keepdims=True)
    yc = y - mu
    var = jnp.mean(yc * yc, axis=-1, keepdims=True)
    return yc * lax.rsqrt(var + LN_EPS) * g + b


def _modulate(x, sh_ref, sc_ref):
    tm, d = x.shape
    g = sh_ref.shape[1]
    xg = x.reshape(g, tm // g, d)
    return (xg * sc_ref[0] + sh_ref[0]).reshape(tm, d)


def _residual_ln(x, f, gate_ref, g_ref, b_ref):
    tm, d = x.shape
    g = gate_ref.shape[1]
    y = ALPHA * x + (gate_ref[0] * f.reshape(g, tm // g, d)).reshape(tm, d)
    return _ln(y, g_ref[...], b_ref[...])


TOKEN_ROWS = 8
LANES = 128
U32 = jnp.uint32


def _pack_halves(y):
    n = y.shape[1] // 2
    lo = lax.bitcast_convert_type(y[:, :n].astype(BF16).astype(F32), U32)
    hi = lax.bitcast_convert_type(y[:, n:].astype(BF16).astype(F32), U32)
    return hi | (lo >> 16)


def _unpack_halves(w):
    lo = lax.bitcast_convert_type(w << 16, F32)
    hi = lax.bitcast_convert_type(w & jnp.uint32(0xFFFF0000), F32)
    return lo, hi


def _store_token_tiles(ref, words):
    tm = words.shape[0]
    for c in range(TOKEN_ROWS):
        ref[pl.ds(c, tm, stride=TOKEN_ROWS), :] = words[:, c * LANES:(c + 1) * LANES]


def _load_token_tiles(ref):
    tm = ref.shape[0] // TOKEN_ROWS
    return jnp.concatenate([ref[pl.ds(c, tm, stride=TOKEN_ROWS), :] for c in range(TOKEN_ROWS)], axis=1)


def _token_tile(ref, i):
    return ref.at[pl.ds(pl.multiple_of(i * TOKEN_ROWS, TOKEN_ROWS), TOKEN_ROWS)]


def _ada_kernel(c_ref, w_ref, b_ref, o_ref, *, d_model, tn):
    j = pl.program_id(1)
    a = jax.nn.silu(c_ref[...]).astype(BF16)
    acc = jnp.dot(a, w_ref[0].astype(BF16), preferred_element_type=F32) + b_ref[0]
    o_ref[0] = acc + jnp.where(j * tn >= d_model, 1.0, 0.0).astype(F32)


def _ada(c_all, w, b):
    depth, d, n = w.shape
    r = c_all.shape[0]
    tn = 1024
    return pl.pallas_call(
        functools.partial(_ada_kernel, d_model=d, tn=tn),
        out_shape=jax.ShapeDtypeStruct((depth, r, n), F32),
        grid=(depth, n // tn),
        in_specs=[
            pl.BlockSpec((r, d), lambda l, j: (0, 0)),
            pl.BlockSpec((1, d, tn), lambda l, j: (l, 0, j)),
            pl.BlockSpec((1, 1, tn), lambda l, j: (l, 0, j)),
        ],
        out_specs=pl.BlockSpec((1, r, tn), lambda l, j: (l, 0, j)),
        compiler_params=_cp(40, ("parallel", "parallel")),
        name="ada",
    )(c_all, w, b.reshape(depth, 1, n))


def _modmm_kernel(x_ref, sh_ref, sc_ref, w_ref, b_ref, o_ref, *, gelu):
    h = _modulate(x_ref[...], sh_ref, sc_ref).astype(BF16)
    acc = jnp.dot(h, w_ref[...], preferred_element_type=F32) + b_ref[...]
    if gelu:
        acc = jax.nn.gelu(acc)
    o_ref[...] = acc.astype(o_ref.dtype)


def _modmm(x, mods, layer, w, b, out_dtype, gelu):
    t, d = x.shape
    n = w.shape[1]
    tm, tn = 512, 2048
    g = tm // GROUP
    return pl.pallas_call(
        functools.partial(_modmm_kernel, gelu=gelu),
        out_shape=jax.ShapeDtypeStruct((t, n), out_dtype),
        grid=(n // tn, t // tm),
        in_specs=[
            pl.BlockSpec((tm, d), lambda j, i: (i, 0)),
            pl.BlockSpec((1, g, 1, d), lambda j, i: (layer, i, 0, 0)),
            pl.BlockSpec((1, g, 1, d), lambda j, i: (layer, i, 0, 1)),
            pl.BlockSpec((d, tn), lambda j, i: (0, j)),
            pl.BlockSpec((1, tn), lambda j, i: (0, j)),
        ],
        out_specs=pl.BlockSpec((tm, tn), lambda j, i: (i, j)),
        compiler_params=_cp(48, ("parallel", "parallel")),
        name="modmm",
    )(x, mods, mods, w, b)


def _qkv_kernel(x_ref, sh_ref, sc_ref, w_ref, *rest):
    qkv_ref, kf_ref, vf_ref = rest[-3:]
    h = _modulate(x_ref[...], sh_ref, sc_ref).astype(BF16)
    acc = jnp.dot(h, w_ref[...], preferred_element_type=F32)
    qkv_ref[...] = acc.astype(BF16)
    tm = acc.shape[0]
    d = acc.shape[1] // 3
    for hd in range(N_HEADS):
        cols = slice(hd * HEAD_DIM, (hd + 1) * HEAD_DIM)
        kf_ref[0, pl.ds(hd, tm, stride=N_HEADS), :] = acc[:, d:2 * d][:, cols]
        vf_ref[0, pl.ds(hd, tm, stride=N_HEADS), :] = acc[:, 2 * d:][:, cols]


def _qkv(x, mods, layer, w, row0, n_rows, slot, n_slots, kf_prev, vf_prev):
    d = x.shape[1]
    tm = 256
    g = tm // GROUP
    off = row0 // tm
    flat = (n_slots, n_rows * N_HEADS, HEAD_DIM)
    in_specs = [
        pl.BlockSpec((tm, d), lambda i: (i + off, 0)),
        pl.BlockSpec((1, g, 1, d), lambda i: (layer, i + off, 0, 0)),
        pl.BlockSpec((1, g, 1, d), lambda i: (layer, i + off, 0, 1)),
        pl.BlockSpec((d, 3 * d), lambda i: (0, 0), pipeline_mode=pl.Buffered(1)),
    ]
    args = [x, mods, mods, w]
    aliases = {}
    if kf_prev is not None:
        in_specs += [pl.BlockSpec(memory_space=pl.ANY), pl.BlockSpec(memory_space=pl.ANY)]
        args += [kf_prev, vf_prev]
        aliases = {4: 1, 5: 2}
    return pl.pallas_call(
        _qkv_kernel,
        out_shape=(jax.ShapeDtypeStruct((n_rows, 3 * d), BF16),
                   jax.ShapeDtypeStruct(flat, F32), jax.ShapeDtypeStruct(flat, F32)),
        grid=(n_rows // tm,),
        in_specs=in_specs,
        out_specs=(pl.BlockSpec((tm, 3 * d), lambda i: (i, 0)),
                   pl.BlockSpec((1, tm * N_HEADS, HEAD_DIM), lambda i: (slot, i, 0)),
                   pl.BlockSpec((1, tm * N_HEADS, HEAD_DIM), lambda i: (slot, i, 0))),
        input_output_aliases=aliases,
        compiler_params=_cp(56, ("parallel",)),
        name="qkv",
    )(*args)


def _sb_block(q, k, v, csum, u, mask):
    z = lax.dot_general(q, k, (((1,), (1,)), ((), ())), preferred_element_type=F32)
    z = z * (HEAD_DIM ** -0.5)
    t = jnp.log(1.0 + jnp.exp(-jnp.abs(z)))
    log_beta = -(jnp.maximum(-z, 0.0) + t)
    l1 = -(jnp.maximum(z, 0.0) + t)
    if mask is not None:
        l1 = jnp.where(mask, l1, 0.0)
    hi = l1.astype(BF16)
    lo = (l1 - hi.astype(F32)).astype(BF16)
    s = jnp.dot(hi, u, preferred_element_type=F32) + jnp.dot(lo, u, preferred_element_type=F32)
    a = jnp.exp(log_beta + s + csum)
    if mask is not None:
        a = jnp.where(mask, a, 0.0)
    o = jnp.dot(a.astype(BF16), v, preferred_element_type=F32)
    return o, csum + s[:, :1] + l1[:, :1]


def _causal_mask(n):
    row = lax.broadcasted_iota(I32, (n, n), 0)
    col = lax.broadcasted_iota(I32, (n, n), 1)
    return col < row


def _attn_prompt_kernel(q_ref, k_ref, v_ref, u_ref, o_ref):
    i = pl.program_id(1)
    bq = q_ref.shape[0]
    heads = q_ref.shape[1] // HEAD_DIM
    u = u_ref[...]
    mask = _causal_mask(bq)

    def kv(b, cols):
        start = pl.multiple_of(b * bq, bq)
        return (k_ref[pl.ds(start, bq), cols].astype(BF16), v_ref[pl.ds(start, bq), cols].astype(BF16))

    prev = jnp.maximum(i - 1, 0)
    has_prev = i > 0
    qs, os, css = [], [], []
    for h in range(heads):
        cols = slice(h * HEAD_DIM, (h + 1) * HEAD_DIM)
        q = q_ref[:, cols].astype(BF16)
        k0, v0 = kv(i, cols)
        o, cs = _sb_block(q, k0, v0, jnp.zeros((bq, 1), F32), u, mask)
        k1, v1 = kv(prev, cols)
        do, cs1 = _sb_block(q, k1, v1, cs, u, None)
        qs.append(q)
        os.append(o + jnp.where(has_prev, do, 0.0))
        css.append(jnp.where(has_prev, cs1, cs))

    def cond(c):
        b, _, css = c
        live = jnp.max(css[0])
        for cs in css[1:]:
            live = jnp.maximum(live, jnp.max(cs))
        return jnp.logical_and(b >= 0, live > STICK_DONE)

    def body(c):
        b, os, css = c
        new_os, new_css = [], []
        for h in range(heads):
            cols = slice(h * HEAD_DIM, (h + 1) * HEAD_DIM)
            kb, vb = kv(b, cols)
            do, cs = _sb_block(qs[h], kb, vb, css[h], u, None)
            new_os.append(os[h] + do)
            new_css.append(cs)
        return b - 1, tuple(new_os), tuple(new_css)

    _, os, _ = lax.while_loop(cond, body, (i - 2, tuple(os), tuple(css)))
    for h in range(heads):
        o_ref[:, h * HEAD_DIM:(h + 1) * HEAD_DIM] = os[h].astype(o_ref.dtype)


def _strict_lower(n):
    r = jnp.arange(n)
    return (r[:, None] > r[None, :]).astype(BF16)


def _attn_prompt(qkv, n_prompt):
    d = N_HEADS * HEAD_DIM
    bq = ATT_BLOCK
    hp = 4
    groups = N_HEADS // hp
    wcols = hp * HEAD_DIM
    return pl.pallas_call(
        _attn_prompt_kernel,
        out_shape=jax.ShapeDtypeStruct((n_prompt, d), BF16),
        grid=(groups, n_prompt // bq),
        in_specs=[
            pl.BlockSpec((bq, wcols), lambda h, i: (i, h)),
            pl.BlockSpec((n_prompt, wcols), lambda h, i: (0, groups + h)),
            pl.BlockSpec((n_prompt, wcols), lambda h, i: (0, 2 * groups + h)),
            pl.BlockSpec((bq, bq), lambda h, i: (0, 0)),
        ],
        out_specs=pl.BlockSpec((bq, wcols), lambda h, i: (i, h)),
        compiler_params=_cp(48, ("parallel", "parallel")),
        name="attn_prompt",
    )(qkv, qkv, qkv, _strict_lower(bq))


def _attn_sample_kernel(q_ref, kn_ref, vn_ref, ck_ref, cv_ref, un_ref, uc_ref, o_ref, cs_ref):
    tq = q_ref.shape[0]
    bk = uc_ref.shape[0]
    mask = _causal_mask(tq)
    for h in range(N_HEADS):
        cols = slice(h * HEAD_DIM, (h + 1) * HEAD_DIM)
        q = q_ref[:, cols].astype(BF16)
        o, cs = _sb_block(q, kn_ref[:, cols].astype(BF16), vn_ref[:, cols].astype(BF16),
                          jnp.zeros((tq, 1), F32), un_ref[...], mask)
        kc = ck_ref[0, 0, pl.ds(h, bk, stride=N_HEADS), :].astype(BF16)
        vc = cv_ref[0, 0, pl.ds(h, bk, stride=N_HEADS), :].astype(BF16)
        do, cs = _sb_block(q, kc, vc, cs, uc_ref[...], None)
        o_ref[:, cols] = o + do
        cs_ref[:, cols] = jnp.broadcast_to(cs, (tq, HEAD_DIM))


def _attn_sample(qkv, cache_k, cache_v, layer, n_prompt, n_streams, t_new):
    d = N_HEADS * HEAD_DIM
    past = cache_k.shape[2] // N_HEADS
    bk = ATT_BLOCK
    pb = n_prompt // t_new
    last = past // bk - 1
    return pl.pallas_call(
        _attn_sample_kernel,
        out_shape=(jax.ShapeDtypeStruct((n_streams * t_new, d), F32),
                   jax.ShapeDtypeStruct((n_streams * t_new, d), F32)),
        grid=(n_streams,),
        in_specs=[
            pl.BlockSpec((t_new, d), lambda s: (pb + s, 0)),
            pl.BlockSpec((t_new, d), lambda s: (pb + s, 1)),
            pl.BlockSpec((t_new, d), lambda s: (pb + s, 2)),
            pl.BlockSpec((1, 1, bk * N_HEADS, HEAD_DIM), lambda s: (layer, s, last, 0)),
            pl.BlockSpec((1, 1, bk * N_HEADS, HEAD_DIM), lambda s: (layer, s, last, 0)),
            pl.BlockSpec((t_new, t_new), lambda s: (0, 0)),
            pl.BlockSpec((bk, bk), lambda s: (0, 0)),
        ],
        out_specs=(pl.BlockSpec((t_new, d), lambda s: (s, 0)),
                   pl.BlockSpec((t_new, d), lambda s: (s, 0))),
        compiler_params=_cp(32, ("parallel",)),
        name="attn_sample",
    )(qkv, qkv, qkv, cache_k, cache_v, _strict_lower(t_new), _strict_lower(bk))


def _attn_sample_rest_kernel(q_ref, oin_ref, csin_ref, ck_ref, cv_ref, u_ref, o_ref, cs_ref):
    b = pl.program_id(1)
    bk = u_ref.shape[0]

    @pl.when(b == 0)
    def _():
        o_ref[...] = oin_ref[...]
        cs_ref[...] = csin_ref[...]

    for h in range(N_HEADS):
        cols = slice(h * HEAD_DIM, (h + 1) * HEAD_DIM)
        cs = cs_ref[:, cols][:, :1]

        @pl.when(jnp.max(cs) > STICK_DONE)
        def _(h=h, cols=cols, cs=cs):
            q = q_ref[:, cols].astype(BF16)
            kc = ck_ref[0, 0, pl.ds(h, bk, stride=N_HEADS), :].astype(BF16)
            vc = cv_ref[0, 0, pl.ds(h, bk, stride=N_HEADS), :].astype(BF16)
            do, cs2 = _sb_block(q, kc, vc, cs, u_ref[...], None)
            o_ref[:, cols] += do
            cs_ref[:, cols] = jnp.broadcast_to(cs2, (cs2.shape[0], HEAD_DIM))


def _attn_sample_rest(qkv, o_part, cs_part, cache_k, cache_v, layer, n_prompt, n_streams, t_new):
    d = N_HEADS * HEAD_DIM
    past = cache_k.shape[2] // N_HEADS
    bk = ATT_BLOCK
    pb = n_prompt // t_new
    nb = past // bk - 1
    o, _ = pl.pallas_call(
        _attn_sample_rest_kernel,
        out_shape=(jax.ShapeDtypeStruct((n_streams * t_new, d), F32),
                   jax.ShapeDtypeStruct((n_streams * t_new, d), F32)),
        grid=(n_streams, nb),
        in_specs=[
            pl.BlockSpec((t_new, d), lambda s, b: (pb + s, 0)),
            pl.BlockSpec((t_new, d), lambda s, b: (s, 0)),
            pl.BlockSpec((t_new, d), lambda s, b: (s, 0)),
            pl.BlockSpec((1, 1, bk * N_HEADS, HEAD_DIM), lambda s, b: (layer, s, nb - 1 - b, 0)),
            pl.BlockSpec((1, 1, bk * N_HEADS, HEAD_DIM), lambda s, b: (layer, s, nb - 1 - b, 0)),
            pl.BlockSpec((bk, bk), lambda s, b: (0, 0)),
        ],
        out_specs=(pl.BlockSpec((t_new, d), lambda s, b: (s, 0)),
                   pl.BlockSpec((t_new, d), lambda s, b: (s, 0))),
        compiler_params=_cp(32, ("parallel", "arbitrary")),
        name="attn_sample_rest",
    )(qkv, o_part, cs_part, cache_k, cache_v, _strict_lower(bk))
    return o


def _proj_ln_kernel(ap_ref, as_ref, x_ref, gate_ref, w_ref, g_ref, b_ref, o_ref, *, prompt_tiles):
    a = jnp.where(pl.program_id(0) < prompt_tiles, ap_ref[...], as_ref[...].astype(BF16))
    f = jnp.dot(a, w_ref[...], preferred_element_type=F32)
    o_ref[...] = _residual_ln(x_ref[...], f, gate_ref, g_ref, b_ref)


def _proj_ln(a_prompt, a_sample, x, mods, layer, w, g, b):
    t, d = x.shape
    kdim = a_prompt.shape[1]
    tm = 256
    ng = tm // GROUP
    pt = a_prompt.shape[0] // tm
    return pl.pallas_call(
        functools.partial(_proj_ln_kernel, prompt_tiles=pt),
        out_shape=jax.ShapeDtypeStruct((t, d), F32),
        grid=(t // tm,),
        in_specs=[
            pl.BlockSpec((tm, kdim), lambda i: (jnp.minimum(i, pt - 1), 0)),
            pl.BlockSpec((tm, kdim), lambda i: (jnp.maximum(i - pt, 0), 0)),
            pl.BlockSpec((tm, d), lambda i: (i, 0)),
            pl.BlockSpec((1, ng, 1, d), lambda i: (layer, i, 0, 2)),
            pl.BlockSpec((kdim, d), lambda i: (0, 0)),
            pl.BlockSpec((1, d), lambda i: (0, 0)),
            pl.BlockSpec((1, d), lambda i: (0, 0)),
        ],
        out_specs=pl.BlockSpec((tm, d), lambda i: (i, 0)),
        compiler_params=_cp(40, ("parallel",)),
        name="proj_ln",
    )(a_prompt, a_sample, x, mods, w, g, b)


def _cm_mix_kernel(u_ref, vraw_ref, x_ref, gate_ref, mx_ref, bias_ref, lvg_ref, lvb_ref,
                   wo_ref, g_ref, b_ref, *rest):
    o_ref, vs_ref, vp_ref, gated_ref = rest[-4:]
    tm, width = u_ref.shape
    gw = width // CM_GROUPS
    v = _ln(vraw_ref[...].astype(F32), lvg_ref[...], lvb_ref[...])
    vs_ref[0] = v
    vp_ref[0] = v
    vb = v.astype(BF16)
    for c in range(tm // CM_CHUNK):
        rows = slice(c * CM_CHUNK, (c + 1) * CM_CHUNK)
        for g in range(CM_GROUPS):
            cols = slice(g * gw, (g + 1) * gw)
            mixed = jnp.dot(mx_ref[0, g], vb[rows, cols], preferred_element_type=F32)
            mixed = mixed + bias_ref[0, :, cols]
            gated_ref[rows, cols] = (u_ref[rows, cols].astype(F32) * mixed).astype(BF16)
    f = jnp.dot(gated_ref[...], wo_ref[...], preferred_element_type=F32)
    o_ref[...] = _residual_ln(x_ref[...], f, gate_ref, g_ref, b_ref)


def _cm_mix(zz, x, mods, layer, mx, bias, lvg, lvb, wo, g, b, n_prompt, slot, n_slots, vs_prev, vp_prev):
    t, d = x.shape
    width = zz.shape[1] // 2
    tm = CM_CHUNK
    ng = tm // GROUP
    n_prompt_tiles = n_prompt // tm
    kind = lambda i: jnp.where(i < n_prompt_tiles, 0, 1)
    extra_specs, extra_args, aliases = [], [], {}
    if vs_prev is not None:
        extra_specs = [pl.BlockSpec(memory_space=pl.ANY), pl.BlockSpec(memory_space=pl.ANY)]
        extra_args = [vs_prev, vp_prev]
        aliases = {11: 1, 12: 2}
    return pl.pallas_call(
        _cm_mix_kernel,
        out_shape=(jax.ShapeDtypeStruct((t, d), F32),
                   jax.ShapeDtypeStruct((n_slots, t - n_prompt, width), F32),
                   jax.ShapeDtypeStruct((n_slots, 2 * tm, width), F32)),
        grid=(t // tm,),
        in_specs=[
            pl.BlockSpec((tm, width), lambda i: (i, 0)),
            pl.BlockSpec((tm, width), lambda i: (i, 1)),
            pl.BlockSpec((tm, d), lambda i: (i, 0)),
            pl.BlockSpec((1, ng, 1, d), lambda i: (layer, i, 0, 2)),
            pl.BlockSpec((1, CM_GROUPS, CM_CHUNK, CM_CHUNK), lambda i: (kind(i), 0, 0, 0)),
            pl.BlockSpec((1, CM_CHUNK, width), lambda i: (kind(i), 0, 0)),
            pl.BlockSpec((1, width), lambda i: (0, 0)),
            pl.BlockSpec((1, width), lambda i: (0, 0)),
            pl.BlockSpec((width, d), lambda i: (0, 0), pipeline_mode=pl.Buffered(1)),
            pl.BlockSpec((1, d), lambda i: (0, 0)),
            pl.BlockSpec((1, d), lambda i: (0, 0)),
        ] + extra_specs,
        out_specs=(pl.BlockSpec((tm, d), lambda i: (i, 0)),
                   pl.BlockSpec((1, tm, width), lambda i: (slot, jnp.maximum(i - n_prompt_tiles, 0), 0)),
                   pl.BlockSpec((1, tm, width), lambda i: (slot, kind(i), 0))),
        scratch_shapes=[pltpu.VMEM((tm, width), BF16)],
        input_output_aliases=aliases,
        compiler_params=_cp(48, ("arbitrary",)),
        name="cm_mix",
    )(zz, zz, x, mods, mx, bias, lvg, lvb, wo, g, b, *extra_args)


def _split3(a):
    hi = a.astype(BF16)
    return hi, (a - hi.astype(F32)).astype(BF16)


def _expert_max(a):
    return jnp.max(jnp.max(a, axis=0, keepdims=True), axis=1, keepdims=True)


def _route_kernel(x_ref, sh_ref, sc_ref, wr_ref, br_ref, xm_ref, cw_ref, mk_ref, cnt_ref):
    i = pl.program_id(0)
    xm = _modulate(x_ref[...], sh_ref, sc_ref)
    _store_token_tiles(xm_ref, _pack_halves(xm))
    xh, xl = _split3(xm)
    wh, wl = _split3(wr_ref[...])
    nt = lambda a, b: lax.dot_general(a, b, (((1,), (1,)), ((), ())), preferred_element_type=F32)
    scores = jax.nn.sigmoid(nt(wh, xh) + (nt(wl, xh) + nt(wh, xl)))
    sel = scores + br_ref[...]
    ne, tm = sel.shape
    ng, per = N_EXPERT_GROUPS, ne // N_EXPERT_GROUPS
    sel3 = sel.reshape(ng, per, tm)
    neg = -jnp.inf
    within = lax.broadcasted_iota(I32, (ng, per, tm), 1)
    m1 = jnp.max(sel3, axis=1, keepdims=True)
    i1 = jnp.min(jnp.where(sel3 == m1, within, per), axis=1, keepdims=True)
    m2 = jnp.max(jnp.where(within == i1, neg, sel3), axis=1, keepdims=True)
    gs = m1 + m2
    gidx = lax.broadcasted_iota(I32, (ng, 1, tm), 0)
    rank = jnp.zeros((ng, 1, tm), I32)
    for g in range(ng):
        row = gs[g:g + 1]
        ahead = jnp.logical_or(row > gs, jnp.logical_and(row == gs, g < gidx))
        rank = rank + ahead.astype(I32)
    allowed = (rank < TOPK_GROUPS).astype(F32)
    ms = jnp.where(allowed > 0.0, sel3, neg)
    eidx = lax.broadcasted_iota(I32, (ng, per, tm), 0) * per + within
    chosen = jnp.zeros((ng, per, tm), F32)
    for _ in range(TOP_K):
        best = _expert_max(ms)
        first = -_expert_max(-jnp.where(ms == best, eidx, ne).astype(F32))
        pick = eidx.astype(F32) == first
        chosen = jnp.where(pick, 1.0, chosen)
        ms = jnp.where(pick, neg, ms)
    chosen = chosen.reshape(ne, tm)
    tw = chosen * scores
    cw_ref[...] = tw / jnp.sum(tw, axis=0, keepdims=True) * ROUTED_SCALE
    mk_ref[...] = chosen.astype(BF16)

    @pl.when(i == 0)
    def _():
        cnt_ref[...] = jnp.zeros_like(cnt_ref)

    cnt_ref[...] += jnp.sum(chosen, axis=1, keepdims=True)


def _route(x, mods, layer, wr_t, br_col):
    t, d = x.shape
    ne = wr_t.shape[0]
    tm = 256
    ng = tm // GROUP
    return pl.pallas_call(
        _route_kernel,
        out_shape=(jax.ShapeDtypeStruct((t * TOKEN_ROWS, LANES), U32), jax.ShapeDtypeStruct((ne, t), F32),
                   jax.ShapeDtypeStruct((ne, t), BF16), jax.ShapeDtypeStruct((ne, 1), F32)),
        grid=(t // tm,),
        in_specs=[
            pl.BlockSpec((tm, d), lambda i: (i, 0)),
            pl.BlockSpec((1, ng, 1, d), lambda i: (layer, i, 0, 0)),
            pl.BlockSpec((1, ng, 1, d), lambda i: (layer, i, 0, 1)),
            pl.BlockSpec((ne, d), lambda i: (0, 0)),
            pl.BlockSpec((ne, 1), lambda i: (0, 0)),
        ],
        out_specs=(pl.BlockSpec((tm * TOKEN_ROWS, LANES), lambda i: (i, 0)),
                   pl.BlockSpec((ne, tm), lambda i: (0, i)),
                   pl.BlockSpec((ne, tm), lambda i: (0, i)), pl.BlockSpec((ne, 1), lambda i: (0, 0))),
        compiler_params=_cp(32, ("arbitrary",)),
        name="router",
    )(x, mods, mods, wr_t, br_col)


def _slots_kernel(mk_ref, cw_ref, start_ref, before_ref, lower_ref, pos_ref, w_ref, carry_ref):
    i = pl.program_id(0)

    @pl.when(i == 0)
    def _():
        carry_ref[...] = jnp.zeros_like(carry_ref)

    mk = mk_ref[...]
    rank = jnp.dot(mk, before_ref[...], preferred_element_type=F32)
    pos = start_ref[...] + carry_ref[...] + rank
    order = jnp.dot(lower_ref[...], mk, preferred_element_type=F32)
    chosen = mk > 0
    cw = cw_ref[...]
    rows_p, rows_w = [], []
    for k in range(TOP_K):
        pick = jnp.logical_and(chosen, order == k)
        rows_p.append(jnp.sum(jnp.where(pick, pos, 0.0), axis=0, keepdims=True))
        rows_w.append(jnp.sum(jnp.where(pick, cw, 0.0), axis=0, keepdims=True))
    pos_ref[...] = jnp.concatenate(rows_p, axis=0).astype(I32)
    w_ref[...] = jnp.concatenate(rows_w, axis=0)
    carry_ref[...] += jnp.sum(mk.astype(F32), axis=1, keepdims=True)


def _slots(mk, cw, start_col):
    ne, t = mk.shape
    tm = 256
    r = jnp.arange(tm)
    before = (r[:, None] < r[None, :]).astype(BF16)
    return pl.pallas_call(
        _slots_kernel,
        out_shape=(jax.ShapeDtypeStruct((TOP_K, t), I32), jax.ShapeDtypeStruct((TOP_K, t), F32)),
        grid=(t // tm,),
        in_specs=[
            pl.BlockSpec((ne, tm), lambda i: (0, i)),
            pl.BlockSpec((ne, tm), lambda i: (0, i)),
            pl.BlockSpec((ne, 1), lambda i: (0, 0)),
            pl.BlockSpec((tm, tm), lambda i: (0, 0)),
            pl.BlockSpec((ne, ne), lambda i: (0, 0)),
        ],
        out_specs=(pl.BlockSpec((TOP_K, tm), lambda i: (0, i)), pl.BlockSpec((TOP_K, tm), lambda i: (0, i))),
        scratch_shapes=[pltpu.VMEM((ne, 1), F32)],
        compiler_params=_cp(32, ("arbitrary",)),
        name="plan",
    )(mk, cw, start_col, before, _strict_lower(ne))


def _dispatch_kernel(fill_ref, pad_ref, pos_ref, xm_ref, xs_ref, zeros_ref, sem):
    i = pl.program_id(0)
    tm = xm_ref.shape[0] // TOKEN_ROWS
    ne = fill_ref.shape[0]
    pad_bits = (zeros_ref.shape[0] // TOKEN_ROWS).bit_length() - 1

    def row_copy(r, k):
        return pltpu.make_async_copy(_token_tile(xm_ref, r), _token_tile(xs_ref, pos_ref[r * TOP_K + k]), sem)

    def fill(e, wait):
        first, n = fill_ref[e], pad_ref[e]
        for bit in range(pad_bits):
            size = (1 << bit) * TOKEN_ROWS
            done = (n >> (bit + 1)) << (bit + 1)
            dst = pl.multiple_of((first + done) * TOKEN_ROWS, TOKEN_ROWS)
            piece = pltpu.make_async_copy(zeros_ref.at[pl.ds(0, size)], xs_ref.at[pl.ds(dst, size)], sem)

            @pl.when(((n >> bit) & 1) == 1)
            def _(piece=piece):
                if wait:
                    piece.wait()
                else:
                    piece.start()

    @pl.when(i == 0)
    def _():
        zeros_ref[...] = jnp.zeros_like(zeros_ref)

        def start(e, c):
            fill(e, False)
            return c

        def wait(e, c):
            fill(e, True)
            return c

        lax.fori_loop(0, ne, start, 0)
        lax.fori_loop(0, ne, wait, 0)

    def start(r, c):
        for k in range(TOP_K):
            row_copy(r, k).start(priority=k % 2)
        return c

    def wait(r, c):
        for k in range(TOP_K):
            row_copy(r, k).wait()
        return c

    lax.fori_loop(0, tm, start, 0)
    lax.fori_loop(0, tm, wait, 0)


def _dispatch(fill, pad, pos_flat, xm, n_slots):
    t = xm.shape[0] // TOKEN_ROWS
    tm = 256
    return pl.pallas_call(
        _dispatch_kernel,
        out_shape=jax.ShapeDtypeStruct((n_slots * TOKEN_ROWS, LANES), U32),
        grid_spec=pltpu.PrefetchScalarGridSpec(
            num_scalar_prefetch=2,
            grid=(t // tm,),
            in_specs=[
                pl.BlockSpec((tm * TOP_K,), lambda i, fill, pad: (i,), memory_space=pltpu.SMEM),
                pl.BlockSpec((tm * TOKEN_ROWS, LANES), lambda i, fill, pad: (i, 0)),
            ],
            out_specs=pl.BlockSpec(memory_space=pl.ANY),
            scratch_shapes=[pltpu.VMEM((EXPERT_TILE * TOKEN_ROWS, LANES), U32), pltpu.SemaphoreType.DMA(())],
        ),
        compiler_params=_cp(32, ("arbitrary",)),
        name="dispatch",
    )(fill, pad, pos_flat, xm)


def _gmm_kernel(te_ref, nt_ref, nxt_ref, slot_ref, xs_ref, wg_ref, wu_ref, wd_ref, ys_ref,
                fg_ref, fu_ref, fd_ref, wgb_ref, wub_ref, wdb_ref, sems, *, layer):
    i = pl.program_id(0)
    e = te_ref[i]
    s = slot_ref[i]
    prev = te_ref[jnp.maximum(i - 1, 0)]

    def fetch(expert, slot):
        return (pltpu.make_async_copy(wg_ref.at[layer, expert], fg_ref.at[slot], sems.at[slot]),
                pltpu.make_async_copy(wu_ref.at[layer, expert], fu_ref.at[slot], sems.at[slot]),
                pltpu.make_async_copy(wd_ref.at[layer, expert], fd_ref.at[slot], sems.at[slot]))

    @pl.when(i == 0)
    def _():
        for c in fetch(e, s):
            c.start()

    @pl.when(jnp.logical_or(i == 0, e != prev))
    def _():
        for c in fetch(e, s):
            c.wait()
        wgb_ref[...] = fg_ref[s].astype(BF16)
        wub_ref[...] = fu_ref[s].astype(BF16)
        wdb_ref[...] = fd_ref[s].astype(BF16)

        @pl.when(nxt_ref[i] >= 0)
        def _():
            for c in fetch(nxt_ref[i], 1 - s):
                c.start()

    @pl.when(i < nt_ref[0])
    def _():
        lo, hi = _unpack_halves(_load_token_tiles(xs_ref))
        x = jnp.concatenate([lo, hi], axis=1).astype(BF16)
        hg = jnp.dot(x, wgb_ref[...], preferred_element_type=F32)
        hu = jnp.dot(x, wub_ref[...], preferred_element_type=F32)
        hid = (jax.nn.silu(hg) * hu).astype(BF16)
        y = jnp.dot(hid, wdb_ref[...], preferred_element_type=F32)
        _store_token_tiles(ys_ref, _pack_halves(y))


def _gmm(tile_expert, n_tiles_used, tile_next, tile_slot, xs, wg, wu, wd, layer, max_tiles):
    d, f = wg.shape[2], wg.shape[3]
    tm = EXPERT_TILE
    row = lambda i, te, nt, nx, sl: (jnp.minimum(i, nt[0] - 1), 0)
    return pl.pallas_call(
        functools.partial(_gmm_kernel, layer=layer),
        out_shape=jax.ShapeDtypeStruct(xs.shape, U32),
        grid_spec=pltpu.PrefetchScalarGridSpec(
            num_scalar_prefetch=4,
            grid=(max_tiles,),
            in_specs=[
                pl.BlockSpec((tm * TOKEN_ROWS, LANES), row),
                pl.BlockSpec(memory_space=pl.ANY),
                pl.BlockSpec(memory_space=pl.ANY),
                pl.BlockSpec(memory_space=pl.ANY),
            ],
            out_specs=pl.BlockSpec((tm * TOKEN_ROWS, LANES), row),
            scratch_shapes=[pltpu.VMEM((2, d, f), F32), pltpu.VMEM((2, d, f), F32), pltpu.VMEM((2, f, d), F32),
                            pltpu.VMEM((d, f), BF16), pltpu.VMEM((d, f), BF16), pltpu.VMEM((f, d), BF16),
                            pltpu.SemaphoreType.DMA((2,))],
        ),
        compiler_params=_cp(56, ("arbitrary",)),
        name="gmm",
    )(tile_expert, n_tiles_used, tile_next, tile_slot, xs, wg, wu, wd)


def _combine_kernel(pos_ref, nxt_ref, w_ref, xm_ref, x_ref, gate_ref, wsg_ref, wsu_ref, wsd_ref,
                    g_ref, b_ref, ys_ref, o_ref, buf_ref, sems):
    i = pl.program_id(0)
    n = pl.num_programs(0)
    tm, d = x_ref.shape
    half = d // 2
    slot = i % 2

    def copy(p_ref, s, r, k):
        return pltpu.make_async_copy(_token_tile(ys_ref, p_ref[r * TOP_K + k]),
                                     _token_tile(buf_ref.at[s, k], r), sems.at[s])

    def gather(p_ref, s):
        def start(r, c):
            for k in range(TOP_K):
                copy(p_ref, s, r, k).start(priority=k % 2)
            return c
        lax.fori_loop(0, tm, start, 0)

    @pl.when(i == 0)
    def _():
        gather(pos_ref, 0)

    for s in range(2):
        @pl.when(jnp.logical_and(i + 1 < n, slot == 1 - s))
        def _(s=s):
            gather(nxt_ref, s)

    lo, hi = _unpack_halves(_load_token_tiles(xm_ref))
    xb = jnp.concatenate([lo, hi], axis=1).astype(BF16)
    hg = jnp.dot(xb, wsg_ref[...], preferred_element_type=F32)
    hu = jnp.dot(xb, wsu_ref[...], preferred_element_type=F32)
    f = jnp.dot((jax.nn.silu(hg) * hu).astype(BF16), wsd_ref[...], preferred_element_type=F32)

    def wait(r, c):
        for k in range(TOP_K):
            copy(pos_ref, slot, r, k).wait()
        return c

    lax.fori_loop(0, tm, wait, 0)
    w = w_ref[...]
    wk = [jnp.broadcast_to(w[:, k:k + 1], (tm, LANES)) for k in range(TOP_K)]
    los, his = [], []
    for c in range(TOKEN_ROWS):
        acc_lo = f[:, c * LANES:(c + 1) * LANES]
        acc_hi = f[:, half + c * LANES:half + (c + 1) * LANES]
        for k in range(TOP_K):
            lo, hi = _unpack_halves(buf_ref[slot, k, pl.ds(c, tm, stride=TOKEN_ROWS), :])
            acc_lo = acc_lo + wk[k] * lo
            acc_hi = acc_hi + wk[k] * hi
        los.append(acc_lo)
        his.append(acc_hi)
    f = jnp.concatenate(los + his, axis=1)
    o_ref[...] = _residual_ln(x_ref[...], f, gate_ref, g_ref, b_ref)


def _combine(pos_flat, w, xm, x, mods, layer, wsg, wsu, wsd, g, b, ys):
    t, d = x.shape
    fs = wsg.shape[1]
    tm = 128
    ng = tm // GROUP
    last = t // tm - 1
    return pl.pallas_call(
        _combine_kernel,
        out_shape=jax.ShapeDtypeStruct((t, d), F32),
        grid=(t // tm,),
        in_specs=[
            pl.BlockSpec((tm * TOP_K,), lambda i: (i,), memory_space=pltpu.SMEM),
            pl.BlockSpec((tm * TOP_K,), lambda i: (jnp.minimum(i + 1, last),), memory_space=pltpu.SMEM),
            pl.BlockSpec((tm, TOP_K), lambda i: (i, 0)),
            pl.BlockSpec((tm * TOKEN_ROWS, LANES), lambda i: (i, 0)),
            pl.BlockSpec((tm, d), lambda i: (i, 0)),
            pl.BlockSpec((1, ng, 1, d), lambda i: (layer, i, 0, 2)),
            pl.BlockSpec((d, fs), lambda i: (0, 0)),
            pl.BlockSpec((d, fs), lambda i: (0, 0)),
            pl.BlockSpec((fs, d), lambda i: (0, 0)),
            pl.BlockSpec((1, d), lambda i: (0, 0)),
            pl.BlockSpec((1, d), lambda i: (0, 0)),
            pl.BlockSpec(memory_space=pl.ANY),
        ],
        out_specs=pl.BlockSpec((tm, d), lambda i: (i, 0)),
        scratch_shapes=[pltpu.VMEM((2, TOP_K, tm * TOKEN_ROWS, LANES), U32), pltpu.SemaphoreType.DMA((2,))],
        compiler_params=_cp(48, ("arbitrary",)),
        name="combine",
    )(pos_flat, pos_flat, w, xm, x, mods, wsg, wsu, wsd, g, b, ys)


def _moe(x, mods, layer, wr, br, wg, wu, wd, wsg, wsu, wsd, g, b):
    t = x.shape[0]
    ne = wr.shape[1]
    tile = EXPERT_TILE
    max_tiles = (t * TOP_K) // tile + ne
    xm, cw, mk, cnt = _route(x, mods, layer, wr.T, br.reshape(ne, 1))
    ids = jnp.arange(ne, dtype=I32)
    upto = ids[None, :] <= ids[:, None]
    cnt = cnt[:, 0].astype(I32)
    tiles = (cnt + tile - 1) // tile
    ends = jnp.sum(jnp.where(upto, tiles[None, :], 0), axis=1)
    start = (ends - tiles) * tile
    n_used = ends[-1:]
    tile_ids = jnp.minimum(jnp.arange(max_tiles, dtype=I32), n_used[0] - 1)
    tile_expert = jnp.sum((ends[None, :] <= tile_ids[:, None]).astype(I32), axis=1)
    used = tiles > 0
    later_used = jnp.logical_and(ids[None, :] > ids[:, None], used[None, :])
    next_used = jnp.min(jnp.where(later_used, ids[None, :], ne), axis=1)
    next_used = jnp.where(next_used < ne, next_used, -1).astype(I32)
    slot = ((jnp.sum(jnp.logical_and(upto, used[None, :]).astype(I32), axis=1) - 1) % 2).astype(I32)
    pos, w = _slots(mk, cw, start.astype(F32)[:, None])
    pos_flat = pos.T.reshape(-1)
    w = w.T
    xs = _dispatch(start + cnt, tiles * tile - cnt, pos_flat, xm, max_tiles * tile)
    ys = _gmm(tile_expert, n_used.astype(I32), next_used[tile_expert], slot[tile_expert],
              xs, wg, wu, wd, layer, max_tiles)
    return _combine(pos_flat, w, xm, x, mods, layer, wsg, wsu, wsd, g, b, ys)


def _expand_mods(mod, n_prompt_groups, n_streams):
    depth, _, n = mod.shape
    p = jnp.broadcast_to(mod[:, 0:1], (depth, n_prompt_groups, n))
    return jnp.concatenate([p, mod[:, 1:1 + n_streams]], axis=1)[:, :, None, :]


def kernel(x_prompt, x_sample, c_prompt, c_sample, cache_k, cache_v, w_ada_mix, b_ada_mix, w_ada_ffn, b_ada_ffn, ln_mix_g, ln_mix_b, ln_ffn_g, ln_ffn_b, w_qkv, w_o, w_uv, b_uv, ln_v_g, ln_v_b, w_s, b_s, w_cm_out, w_router, b_router, w_gate, w_up, w_down, ws_gate, ws_up, ws_down):
    bp, sp, d = x_prompt.shape
    bs, t_new, _ = x_sample.shape
    assert bp == 1 and t_new == GROUP and sp % ATT_BLOCK == 0 and d == N_HEADS * HEAD_DIM
    n_prompt = bp * sp
    n_sample = bs * t_new
    width = w_uv.shape[2] // 2
    past = cache_k.shape[2]

    x = jnp.concatenate([x_prompt.reshape(n_prompt, d), x_sample.reshape(n_sample, d)], axis=0)

    rows = 8 * ((1 + bs + 7) // 8)
    c_all = jnp.zeros((rows, d), F32).at[0:1].set(c_prompt).at[1:1 + bs].set(c_sample)
    mods_mix = _expand_mods(_ada(c_all, w_ada_mix, b_ada_mix), n_prompt // GROUP, bs)
    mods_ffn = _expand_mods(_ada(c_all, w_ada_ffn, b_ada_ffn), n_prompt // GROUP, bs)

    cpos = jnp.arange(CM_CHUNK)
    cmask = (cpos[None, :] // CHUNK) <= (cpos[:, None] // CHUNK)
    pair = (cpos[None, :] // t_new) == (cpos[:, None] // t_new)
    fold = cpos % t_new
    smask = (fold[None, :] // CHUNK) <= (fold[:, None] // CHUNK)
    ws_prompt = jnp.where(cmask, w_s, 0.0)
    ws_sample = jnp.where(jnp.logical_and(pair, smask), w_s[:, :, fold][:, :, :, fold], 0.0)
    mix = jnp.stack([ws_prompt, ws_sample], axis=1).astype(BF16)
    gw = width // CM_GROUPS
    bias_p = jnp.repeat(jnp.swapaxes(b_s, 1, 2), gw, axis=2)
    mix_bias = jnp.stack([bias_p, bias_p[:, fold]], axis=1)

    ck = cache_k.reshape(cache_k.shape[0], bs, past * N_HEADS, HEAD_DIM)
    cv = cache_v.reshape(cache_v.shape[0], bs, past * N_HEADS, HEAD_DIM)

    n_sb = (DEPTH + 1) // 2
    n_cm = DEPTH // 2
    kp = vp = ksm = vsm = cms = cmp_ = None
    for i in range(DEPTH):
        j = i // 2
        if i % 2 == 0:
            wq = w_qkv[j].astype(BF16)
            qkv_p, kp, vp = _qkv(x, mods_mix, i, wq, 0, n_prompt, j, n_sb, kp, vp)
            qkv_s, ksm, vsm = _qkv(x, mods_mix, i, wq, n_prompt, n_sample, j, n_sb, ksm, vsm)
            o_p = _attn_prompt(qkv_p, n_prompt)
            o_s, cs = _attn_sample(qkv_s, ck, cv, j, 0, bs, t_new)
            o_s = lax.cond(
                jnp.max(cs) > STICK_DONE,
                lambda: _attn_sample_rest(qkv_s, o_s, cs, ck, cv, j, 0, bs, t_new),
                lambda: o_s)
            x = _proj_ln(o_p, o_s, x, mods_mix, i, w_o[j].astype(BF16), ln_mix_g[i][None], ln_mix_b[i][None])
        else:
            zz = _modmm(x, mods_mix, i, w_uv[j].astype(BF16), b_uv[j][None], BF16, True)
            x, cms, cmp_ = _cm_mix(zz, x, mods_mix, i, mix[j], mix_bias[j], ln_v_g[j][None], ln_v_b[j][None],
                                   w_cm_out[j].astype(BF16), ln_mix_g[i][None], ln_mix_b[i][None],
                                   n_prompt, j, n_cm, cms, cmp_)
        x = _moe(x, mods_ffn, i, w_router[i], b_router[i][None], w_gate, w_up, w_down,
                 ws_gate[i].astype(BF16), ws_up[i].astype(BF16), ws_down[i].astype(BF16),
                 ln_ffn_g[i][None], ln_ffn_b[i][None])

    return (
        x[:n_prompt].reshape(bp, sp, d),
        x[n_prompt:].reshape(bs, t_new, d),
        kp.reshape(n_sb, bp, sp, N_HEADS, HEAD_DIM),
        vp.reshape(n_sb, bp, sp, N_HEADS, HEAD_DIM),
        ksm.reshape(n_sb, bs, t_new, N_HEADS, HEAD_DIM),
        vsm.reshape(n_sb, bs, t_new, N_HEADS, HEAD_DIM),
        cmp_[:, :CM_CHUNK].reshape(n_cm, bp, CM_CHUNK, width),
        cms.reshape(n_cm, bs, t_new, width),
    )
```

```python
import functools

import jax
import jax.numpy as jnp
from jax import lax
from jax.experimental import pallas as pl
from jax.experimental.pallas import tpu as pltpu

F32 = jnp.float32
BF16 = jnp.bfloat16
I32 = jnp.int32

DEPTH = 4
N_HEADS = 16
HEAD_DIM = 128
CHUNK = 64
CM_CHUNK = 128
CM_GROUPS = 16
N_EXPERTS = 64
N_EXPERT_GROUPS = 8
TOPK_GROUPS = 4
TOP_K = 8
ROUTED_SCALE = 2.5
ALPHA = (2 * DEPTH) ** 0.25
LN_EPS = 1e-5

GROUP = 64
MIB = 1024 * 1024
STICK_DONE = -104.0
ATT_BLOCK = 256
EXPERT_TILE = 256


def _cp(vmem_mib, sem):
    return pltpu.CompilerParams(dimension_semantics=sem, vmem_limit_bytes=vmem_mib * MIB)


def _ln(y, g, b):
    mu = jnp.mean(y, axis=-1, keepdims=True)
    yc = y - mu
    var = jnp.mean(yc * yc, axis=-1, keepdims=True)
    return yc * lax.rsqrt(var + LN_EPS) * g + b


def _modulate(x, sh_ref, sc_ref):
    tm, d = x.shape
    g = sh_ref.shape[1]
    xg = x.reshape(g, tm // g, d)
    return (xg * sc_ref[0] + sh_ref[0]).reshape(tm, d)


def _residual_ln(x, f, gate_ref, g_ref, b_ref):
    tm, d = x.shape
    g = gate_ref.shape[1]
    y = ALPHA * x + (gate_ref[0] * f.reshape(g, tm // g, d)).reshape(tm, d)
    return _ln(y, g_ref[...], b_ref[...])


TOKEN_ROWS = 8
LANES = 128
U32 = jnp.uint32


def _pack_halves(y):
    n = y.shape[1] // 2
    lo = lax.bitcast_convert_type(y[:, :n].astype(BF16).astype(F32), U32)
    hi = lax.bitcast_convert_type(y[:, n:].astype(BF16).astype(F32), U32)
    return hi | (lo >> 16)


def _unpack_halves(w):
    lo = lax.bitcast_convert_type(w << 16, F32)
    hi = lax.bitcast_convert_type(w & jnp.uint32(0xFFFF0000), F32)
    return lo, hi


def _store_token_tiles(ref, words):
    tm = words.shape[0]
    for c in range(TOKEN_ROWS):
        ref[pl.ds(c, tm, stride=TOKEN_ROWS), :] = words[:, c * LANES:(c + 1) * LANES]


def _load_token_tiles(ref):
    tm = ref.shape[0] // TOKEN_ROWS
    return jnp.concatenate([ref[pl.ds(c, tm, stride=TOKEN_ROWS), :] for c in range(TOKEN_ROWS)], axis=1)


def _token_tile(ref, i):
    return ref.at[pl.ds(pl.multiple_of(i * TOKEN_ROWS, TOKEN_ROWS), TOKEN_ROWS)]


def _ada_kernel(c_ref, w_ref, b_ref, o_ref, *, d_model, tn):
    j = pl.program_id(1)
    a = jax.nn.silu(c_ref[...]).astype(BF16)
    acc = jnp.dot(a, w_ref[0].astype(BF16), preferred_element_type=F32) + b_ref[0]
    o_ref[0] = acc + jnp.where(j * tn >= d_model, 1.0, 0.0).astype(F32)


def _ada(c_all, w, b):
    depth, d, n = w.shape
    r = c_all.shape[0]
    tn = 1024
    return pl.pallas_call(
        functools.partial(_ada_kernel, d_model=d, tn=tn),
        out_shape=jax.ShapeDtypeStruct((depth, r, n), F32),
        grid=(depth, n // tn),
        in_specs=[
            pl.BlockSpec((r, d), lambda l, j: (0, 0)),
            pl.BlockSpec((1, d, tn), lambda l, j: (l, 0, j)),
            pl.BlockSpec((1, 1, tn), lambda l, j: (l, 0, j)),
        ],
        out_specs=pl.BlockSpec((1, r, tn), lambda l, j: (l, 0, j)),
        compiler_params=_cp(40, ("parallel", "parallel")),
        name="ada",
    )(c_all, w, b.reshape(depth, 1, n))


def _modmm_kernel(x_ref, sh_ref, sc_ref, w_ref, b_ref, o_ref, *, gelu):
    h = _modulate(x_ref[...], sh_ref, sc_ref).astype(BF16)
    acc = jnp.dot(h, w_ref[...], preferred_element_type=F32) + b_ref[...]
    if gelu:
        acc = jax.nn.gelu(acc)
    o_ref[...] = acc.astype(o_ref.dtype)


def _modmm(x, mods, layer, w, b, wslot, out_dtype, gelu):
    t, d = x.shape
    n = w.shape[2]
    tm, tn = 512, 2048
    g = tm // GROUP
    return pl.pallas_call(
        functools.partial(_modmm_kernel, gelu=gelu),
        out_shape=jax.ShapeDtypeStruct((t, n), out_dtype),
        grid=(n // tn, t // tm),
        in_specs=[
            pl.BlockSpec((tm, d), lambda j, i: (i, 0)),
            pl.BlockSpec((1, g, 1, d), lambda j, i: (layer, i, 0, 0)),
            pl.BlockSpec((1, g, 1, d), lambda j, i: (layer, i, 0, 1)),
            pl.BlockSpec((None, d, tn), lambda j, i: (wslot, 0, j)),
            pl.BlockSpec((None, 1, tn), lambda j, i: (wslot, 0, j)),
        ],
        out_specs=pl.BlockSpec((tm, tn), lambda j, i: (i, j)),
        compiler_params=_cp(48, ("parallel", "parallel")),
        name="modmm",
    )(x, mods, mods, w, b)


def _qkv_kernel(x_ref, sh_ref, sc_ref, w_ref, *rest):
    qkv_ref, kf_ref, vf_ref = rest[-3:]
    h = _modulate(x_ref[...], sh_ref, sc_ref).astype(BF16)
    acc = jnp.dot(h, w_ref[...], preferred_element_type=F32)
    qkv_ref[...] = acc.astype(BF16)
    tm = acc.shape[0]
    d = acc.shape[1] // 3
    for hd in range(N_HEADS):
        cols = slice(hd * HEAD_DIM, (hd + 1) * HEAD_DIM)
        kf_ref[0, pl.ds(hd, tm, stride=N_HEADS), :] = acc[:, d:2 * d][:, cols]
        vf_ref[0, pl.ds(hd, tm, stride=N_HEADS), :] = acc[:, 2 * d:][:, cols]


def _qkv(x, mods, layer, w, row0, n_rows, slot, n_slots, kf_prev, vf_prev):
    d = x.shape[1]
    tm = 256
    g = tm // GROUP
    off = row0 // tm
    flat = (n_slots, n_rows * N_HEADS, HEAD_DIM)
    in_specs = [
        pl.BlockSpec((tm, d), lambda i: (i + off, 0)),
        pl.BlockSpec((1, g, 1, d), lambda i: (layer, i + off, 0, 0)),
        pl.BlockSpec((1, g, 1, d), lambda i: (layer, i + off, 0, 1)),
        pl.BlockSpec((None, d, 3 * d), lambda i: (slot, 0, 0), pipeline_mode=pl.Buffered(1)),
    ]
    args = [x, mods, mods, w]
    aliases = {}
    if kf_prev is not None:
        in_specs += [pl.BlockSpec(memory_space=pl.ANY), pl.BlockSpec(memory_space=pl.ANY)]
        args += [kf_prev, vf_prev]
        aliases = {4: 1, 5: 2}
    return pl.pallas_call(
        _qkv_kernel,
        out_shape=(jax.ShapeDtypeStruct((n_rows, 3 * d), BF16),
                   jax.ShapeDtypeStruct(flat, F32), jax.ShapeDtypeStruct(flat, F32)),
        grid=(n_rows // tm,),
        in_specs=in_specs,
        out_specs=(pl.BlockSpec((tm, 3 * d), lambda i: (i, 0)),
                   pl.BlockSpec((1, tm * N_HEADS, HEAD_DIM), lambda i: (slot, i, 0)),
                   pl.BlockSpec((1, tm * N_HEADS, HEAD_DIM), lambda i: (slot, i, 0))),
        input_output_aliases=aliases,
        compiler_params=_cp(56, ("parallel",)),
        name="qkv",
    )(*args)


def _sb_block(q, k, v, csum, u, mask):
    z = lax.dot_general(q, k, (((1,), (1,)), ((), ())), preferred_element_type=F32)
    z = z * (HEAD_DIM ** -0.5)
    t = jnp.log(1.0 + jnp.exp(-jnp.abs(z)))
    log_beta = -(jnp.maximum(-z, 0.0) + t)
    l1 = log_beta - z
    if mask is not None:
        l1 = jnp.where(mask, l1, 0.0)
    hi = l1.astype(BF16)
    lo = (l1 - hi.astype(F32)).astype(BF16)
    s = jnp.dot(hi, u, preferred_element_type=F32) + jnp.dot(lo, u, preferred_element_type=F32)
    a = jnp.exp(log_beta + s + csum)
    if mask is not None:
        a = jnp.where(mask, a, 0.0)
    o = jnp.dot(a.astype(BF16), v, preferred_element_type=F32)
    return o, csum + s[:, :1] + l1[:, :1]


def _causal_mask(n):
    row = lax.broadcasted_iota(I32, (n, n), 0)
    col = lax.broadcasted_iota(I32, (n, n), 1)
    return col < row


def _attn_prompt_kernel(q_ref, k_ref, v_ref, u_ref, o_ref):
    i = pl.program_id(1)
    bq = q_ref.shape[0]
    heads = q_ref.shape[1] // HEAD_DIM
    u = u_ref[...]
    mask = _causal_mask(bq)

    def kv(b, cols):
        start = pl.multiple_of(b * bq, bq)
        return (k_ref[pl.ds(start, bq), cols].astype(BF16), v_ref[pl.ds(start, bq), cols].astype(BF16))

    prev = jnp.maximum(i - 1, 0)
    has_prev = i > 0
    qs, os, css = [], [], []
    for h in range(heads):
        cols = slice(h * HEAD_DIM, (h + 1) * HEAD_DIM)
        q = q_ref[:, cols].astype(BF16)
        k0, v0 = kv(i, cols)
        o, cs = _sb_block(q, k0, v0, jnp.zeros((bq, 1), F32), u, mask)
        k1, v1 = kv(prev, cols)
        do, cs1 = _sb_block(q, k1, v1, cs, u, None)
        qs.append(q)
        os.append(o + jnp.where(has_prev, do, 0.0))
        css.append(jnp.where(has_prev, cs1, cs))

    def cond(c):
        b, _, css = c
        live = jnp.max(css[0])
        for cs in css[1:]:
            live = jnp.maximum(live, jnp.max(cs))
        return jnp.logical_and(b >= 0, live > STICK_DONE)

    def body(c):
        b, os, css = c
        new_os, new_css = [], []
        for h in range(heads):
            cols = slice(h * HEAD_DIM, (h + 1) * HEAD_DIM)
            kb, vb = kv(b, cols)
            do, cs = _sb_block(qs[h], kb, vb, css[h], u, None)
            new_os.append(os[h] + do)
            new_css.append(cs)
        return b - 1, tuple(new_os), tuple(new_css)

    _, os, _ = lax.while_loop(cond, body, (i - 2, tuple(os), tuple(css)))
    for h in range(heads):
        o_ref[:, h * HEAD_DIM:(h + 1) * HEAD_DIM] = os[h].astype(o_ref.dtype)


def _strict_lower(n):
    r = jnp.arange(n)
    return (r[:, None] > r[None, :]).astype(BF16)


def _attn_prompt(qkv, n_prompt):
    d = N_HEADS * HEAD_DIM
    bq = ATT_BLOCK
    hp = 4
    groups = N_HEADS // hp
    wcols = hp * HEAD_DIM
    return pl.pallas_call(
        _attn_prompt_kernel,
        out_shape=jax.ShapeDtypeStruct((n_prompt, d), BF16),
        grid=(groups, n_prompt // bq),
        in_specs=[
            pl.BlockSpec((bq, wcols), lambda h, i: (i, h)),
            pl.BlockSpec((n_prompt, wcols), lambda h, i: (0, groups + h)),
            pl.BlockSpec((n_prompt, wcols), lambda h, i: (0, 2 * groups + h)),
            pl.BlockSpec((bq, bq), lambda h, i: (0, 0)),
        ],
        out_specs=pl.BlockSpec((bq, wcols), lambda h, i: (i, h)),
        compiler_params=_cp(48, ("parallel", "parallel")),
        name="attn_prompt",
    )(qkv, qkv, qkv, _strict_lower(bq))


def _attn_sample_kernel(q_ref, kn_ref, vn_ref, ck_ref, cv_ref, un_ref, uc_ref, o_ref, cs_ref):
    tq = q_ref.shape[0]
    bk = uc_ref.shape[0]
    mask = _causal_mask(tq)
    for h in range(N_HEADS):
        cols = slice(h * HEAD_DIM, (h + 1) * HEAD_DIM)
        q = q_ref[:, cols].astype(BF16)
        o, cs = _sb_block(q, kn_ref[:, cols].astype(BF16), vn_ref[:, cols].astype(BF16),
                          jnp.zeros((tq, 1), F32), un_ref[...], mask)
        kc = ck_ref[0, 0, pl.ds(h, bk, stride=N_HEADS), :].astype(BF16)
        vc = cv_ref[0, 0, pl.ds(h, bk, stride=N_HEADS), :].astype(BF16)
        do, cs = _sb_block(q, kc, vc, cs, uc_ref[...], None)
        o_ref[:, cols] = o + do
        cs_ref[:, cols] = jnp.broadcast_to(cs, (tq, HEAD_DIM))


def _attn_sample(qkv, cache_k, cache_v, layer, n_prompt, n_streams, t_new):
    d = N_HEADS * HEAD_DIM
    past = cache_k.shape[2] // N_HEADS
    bk = ATT_BLOCK
    pb = n_prompt // t_new
    last = past // bk - 1
    return pl.pallas_call(
        _attn_sample_kernel,
        out_shape=(jax.ShapeDtypeStruct((n_streams * t_new, d), F32),
                   jax.ShapeDtypeStruct((n_streams * t_new, d), F32)),
        grid=(n_streams,),
        in_specs=[
            pl.BlockSpec((t_new, d), lambda s: (pb + s, 0)),
            pl.BlockSpec((t_new, d), lambda s: (pb + s, 1)),
            pl.BlockSpec((t_new, d), lambda s: (pb + s, 2)),
            pl.BlockSpec((1, 1, bk * N_HEADS, HEAD_DIM), lambda s: (layer, s, last, 0)),
            pl.BlockSpec((1, 1, bk * N_HEADS, HEAD_DIM), lambda s: (layer, s, last, 0)),
            pl.BlockSpec((t_new, t_new), lambda s: (0, 0)),
            pl.BlockSpec((bk, bk), lambda s: (0, 0)),
        ],
        out_specs=(pl.BlockSpec((t_new, d), lambda s: (s, 0)),
                   pl.BlockSpec((t_new, d), lambda s: (s, 0))),
        compiler_params=_cp(32, ("parallel",)),
        name="attn_sample",
    )(qkv, qkv, qkv, cache_k, cache_v, _strict_lower(t_new), _strict_lower(bk))


def _attn_sample_rest_kernel(q_ref, oin_ref, csin_ref, ck_ref, cv_ref, u_ref, o_ref, cs_ref):
    b = pl.program_id(1)
    bk = u_ref.shape[0]

    @pl.when(b == 0)
    def _():
        o_ref[...] = oin_ref[...]
        cs_ref[...] = csin_ref[...]

    for h in range(N_HEADS):
        cols = slice(h * HEAD_DIM, (h + 1) * HEAD_DIM)
        cs = cs_ref[:, cols][:, :1]

        @pl.when(jnp.max(cs) > STICK_DONE)
        def _(h=h, cols=cols, cs=cs):
            q = q_ref[:, cols].astype(BF16)
            kc = ck_ref[0, 0, pl.ds(h, bk, stride=N_HEADS), :].astype(BF16)
            vc = cv_ref[0, 0, pl.ds(h, bk, stride=N_HEADS), :].astype(BF16)
            do, cs2 = _sb_block(q, kc, vc, cs, u_ref[...], None)
            o_ref[:, cols] += do
            cs_ref[:, cols] = jnp.broadcast_to(cs2, (cs2.shape[0], HEAD_DIM))


def _attn_sample_rest(qkv, o_part, cs_part, cache_k, cache_v, layer, n_prompt, n_streams, t_new):
    d = N_HEADS * HEAD_DIM
    past = cache_k.shape[2] // N_HEADS
    bk = ATT_BLOCK
    pb = n_prompt // t_new
    nb = past // bk - 1
    o, _ = pl.pallas_call(
        _attn_sample_rest_kernel,
        out_shape=(jax.ShapeDtypeStruct((n_streams * t_new, d), F32),
                   jax.ShapeDtypeStruct((n_streams * t_new, d), F32)),
        grid=(n_streams, nb),
        in_specs=[
            pl.BlockSpec((t_new, d), lambda s, b: (pb + s, 0)),
            pl.BlockSpec((t_new, d), lambda s, b: (s, 0)),
            pl.BlockSpec((t_new, d), lambda s, b: (s, 0)),
            pl.BlockSpec((1, 1, bk * N_HEADS, HEAD_DIM), lambda s, b: (layer, s, nb - 1 - b, 0)),
            pl.BlockSpec((1, 1, bk * N_HEADS, HEAD_DIM), lambda s, b: (layer, s, nb - 1 - b, 0)),
            pl.BlockSpec((bk, bk), lambda s, b: (0, 0)),
        ],
        out_specs=(pl.BlockSpec((t_new, d), lambda s, b: (s, 0)),
                   pl.BlockSpec((t_new, d), lambda s, b: (s, 0))),
        compiler_params=_cp(32, ("parallel", "arbitrary")),
        name="attn_sample_rest",
    )(qkv, o_part, cs_part, cache_k, cache_v, _strict_lower(bk))
    return o


def _proj_ln_kernel(ap_ref, as_ref, x_ref, gate_ref, w_ref, g_ref, b_ref, o_ref, *, prompt_tiles):
    a = jnp.where(pl.program_id(0) < prompt_tiles, ap_ref[...], as_ref[...].astype(BF16))
    f = jnp.dot(a, w_ref[...], preferred_element_type=F32)
    o_ref[...] = _residual_ln(x_ref[...], f, gate_ref, g_ref, b_ref)


def _proj_ln(a_prompt, a_sample, x, mods, layer, w, wslot, g, b):
    t, d = x.shape
    kdim = a_prompt.shape[1]
    tm = 256
    ng = tm // GROUP
    pt = a_prompt.shape[0] // tm
    return pl.pallas_call(
        functools.partial(_proj_ln_kernel, prompt_tiles=pt),
        out_shape=jax.ShapeDtypeStruct((t, d), F32),
        grid=(t // tm,),
        in_specs=[
            pl.BlockSpec((tm, kdim), lambda i: (jnp.minimum(i, pt - 1), 0)),
            pl.BlockSpec((tm, kdim), lambda i: (jnp.maximum(i - pt, 0), 0)),
            pl.BlockSpec((tm, d), lambda i: (i, 0)),
            pl.BlockSpec((1, ng, 1, d), lambda i: (layer, i, 0, 2)),
            pl.BlockSpec((None, kdim, d), lambda i: (wslot, 0, 0)),
            pl.BlockSpec((1, d), lambda i: (0, 0)),
            pl.BlockSpec((1, d), lambda i: (0, 0)),
        ],
        out_specs=pl.BlockSpec((tm, d), lambda i: (i, 0)),
        compiler_params=_cp(40, ("parallel",)),
        name="proj_ln",
    )(a_prompt, a_sample, x, mods, w, g, b)


def _cm_mix_kernel(u_ref, vraw_ref, x_ref, gate_ref, mx_ref, bias_ref, lvg_ref, lvb_ref,
                   wo_ref, g_ref, b_ref, *rest):
    o_ref, vs_ref, vp_ref, gated_ref = rest[-4:]
    tm, width = u_ref.shape
    gw = width // CM_GROUPS
    v = _ln(vraw_ref[...].astype(F32), lvg_ref[...], lvb_ref[...])
    vs_ref[0] = v
    vp_ref[0] = v
    vb = v.astype(BF16)
    for c in range(tm // CM_CHUNK):
        rows = slice(c * CM_CHUNK, (c + 1) * CM_CHUNK)
        for g in range(CM_GROUPS):
            cols = slice(g * gw, (g + 1) * gw)
            mixed = jnp.dot(mx_ref[0, g], vb[rows, cols], preferred_element_type=F32)
            mixed = mixed + bias_ref[0, :, cols]
            gated_ref[rows, cols] = (u_ref[rows, cols].astype(F32) * mixed).astype(BF16)
    f = jnp.dot(gated_ref[...], wo_ref[...], preferred_element_type=F32)
    o_ref[...] = _residual_ln(x_ref[...], f, gate_ref, g_ref, b_ref)


def _cm_mix(zz, x, mods, layer, mx, bias, lvg, lvb, wo, g, b, n_prompt, slot, n_slots, vs_prev, vp_prev):
    t, d = x.shape
    width = zz.shape[1] // 2
    tm = CM_CHUNK
    ng = tm // GROUP
    n_prompt_tiles = n_prompt // tm
    kind = lambda i: jnp.where(i < n_prompt_tiles, 0, 1)
    extra_specs, extra_args, aliases = [], [], {}
    if vs_prev is not None:
        extra_specs = [pl.BlockSpec(memory_space=pl.ANY), pl.BlockSpec(memory_space=pl.ANY)]
        extra_args = [vs_prev, vp_prev]
        aliases = {11: 1, 12: 2}
    return pl.pallas_call(
        _cm_mix_kernel,
        out_shape=(jax.ShapeDtypeStruct((t, d), F32),
                   jax.ShapeDtypeStruct((n_slots, t - n_prompt, width), F32),
                   jax.ShapeDtypeStruct((n_slots, 2 * tm, width), F32)),
        grid=(t // tm,),
        in_specs=[
            pl.BlockSpec((tm, width), lambda i: (i, 0)),
            pl.BlockSpec((tm, width), lambda i: (i, 1)),
            pl.BlockSpec((tm, d), lambda i: (i, 0)),
            pl.BlockSpec((1, ng, 1, d), lambda i: (layer, i, 0, 2)),
            pl.BlockSpec((1, CM_GROUPS, CM_CHUNK, CM_CHUNK), lambda i: (kind(i), 0, 0, 0)),
            pl.BlockSpec((1, CM_CHUNK, width), lambda i: (kind(i), 0, 0)),
            pl.BlockSpec((1, width), lambda i: (0, 0)),
            pl.BlockSpec((1, width), lambda i: (0, 0)),
            pl.BlockSpec((None, width, d), lambda i: (slot, 0, 0), pipeline_mode=pl.Buffered(1)),
            pl.BlockSpec((1, d), lambda i: (0, 0)),
            pl.BlockSpec((1, d), lambda i: (0, 0)),
        ] + extra_specs,
        out_specs=(pl.BlockSpec((tm, d), lambda i: (i, 0)),
                   pl.BlockSpec((1, tm, width), lambda i: (slot, jnp.maximum(i - n_prompt_tiles, 0), 0)),
                   pl.BlockSpec((1, tm, width), lambda i: (slot, kind(i), 0))),
        scratch_shapes=[pltpu.VMEM((tm, width), BF16)],
        input_output_aliases=aliases,
        compiler_params=_cp(48, ("arbitrary",)),
        name="cm_mix",
    )(zz, zz, x, mods, mx, bias, lvg, lvb, wo, g, b, *extra_args)


def _split3(a):
    hi = a.astype(BF16)
    return hi, (a - hi.astype(F32)).astype(BF16)


def _expert_max(a):
    return jnp.max(jnp.max(a, axis=0, keepdims=True), axis=1, keepdims=True)


def _route_kernel(x_ref, sh_ref, sc_ref, wr_ref, br_ref, xm_ref, cw_ref, mk_ref, cnt_ref):
    i = pl.program_id(0)
    xm = _modulate(x_ref[...], sh_ref, sc_ref)
    _store_token_tiles(xm_ref, _pack_halves(xm))
    xh, xl = _split3(xm)
    wh, wl = _split3(wr_ref[...])
    nt = lambda a, b: lax.dot_general(a, b, (((1,), (1,)), ((), ())), preferred_element_type=F32)
    scores = jax.nn.sigmoid(nt(wh, xh) + (nt(wl, xh) + nt(wh, xl)))
    sel = scores + br_ref[...]
    ne, tm = sel.shape
    ng, per = N_EXPERT_GROUPS, ne // N_EXPERT_GROUPS
    sel3 = sel.reshape(ng, per, tm)
    neg = -jnp.inf
    within = lax.broadcasted_iota(I32, (ng, per, tm), 1)
    m1 = jnp.max(sel3, axis=1, keepdims=True)
    i1 = jnp.min(jnp.where(sel3 == m1, within, per), axis=1, keepdims=True)
    m2 = jnp.max(jnp.where(within == i1, neg, sel3), axis=1, keepdims=True)
    gs = m1 + m2
    gidx = lax.broadcasted_iota(I32, (ng, 1, tm), 0)
    rank = jnp.zeros((ng, 1, tm), I32)
    for g in range(ng):
        row = gs[g:g + 1]
        ahead = jnp.logical_or(row > gs, jnp.logical_and(row == gs, g < gidx))
        rank = rank + ahead.astype(I32)
    allowed = (rank < TOPK_GROUPS).astype(F32)
    ms = jnp.where(allowed > 0.0, sel3, neg)
    eidx = lax.broadcasted_iota(I32, (ng, per, tm), 0) * per + within
    chosen = jnp.zeros((ng, per, tm), F32)
    for _ in range(TOP_K):
        best = _expert_max(ms)
        first = -_expert_max(-jnp.where(ms == best, eidx, ne).astype(F32))
        pick = eidx.astype(F32) == first
        chosen = jnp.where(pick, 1.0, chosen)
        ms = jnp.where(pick, neg, ms)
    chosen = chosen.reshape(ne, tm)
    tw = chosen * scores
    cw_ref[...] = tw / jnp.sum(tw, axis=0, keepdims=True) * ROUTED_SCALE
    mk_ref[...] = chosen.astype(BF16)

    @pl.when(i == 0)
    def _():
        cnt_ref[...] = jnp.zeros_like(cnt_ref)

    cnt_ref[...] += jnp.sum(chosen, axis=1, keepdims=True)


def _route(x, mods, layer, wr_t, br_col):
    t, d = x.shape
    ne = wr_t.shape[0]
    tm = 256
    ng = tm // GROUP
    return pl.pallas_call(
        _route_kernel,
        out_shape=(jax.ShapeDtypeStruct((t * TOKEN_ROWS, LANES), U32), jax.ShapeDtypeStruct((ne, t), F32),
                   jax.ShapeDtypeStruct((ne, t), BF16), jax.ShapeDtypeStruct((ne, 1), F32)),
        grid=(t // tm,),
        in_specs=[
            pl.BlockSpec((tm, d), lambda i: (i, 0)),
            pl.BlockSpec((1, ng, 1, d), lambda i: (layer, i, 0, 0)),
            pl.BlockSpec((1, ng, 1, d), lambda i: (layer, i, 0, 1)),
            pl.BlockSpec((ne, d), lambda i: (0, 0)),
            pl.BlockSpec((ne, 1), lambda i: (0, 0)),
        ],
        out_specs=(pl.BlockSpec((tm * TOKEN_ROWS, LANES), lambda i: (i, 0)),
                   pl.BlockSpec((ne, tm), lambda i: (0, i)),
                   pl.BlockSpec((ne, tm), lambda i: (0, i)), pl.BlockSpec((ne, 1), lambda i: (0, 0))),
        compiler_params=_cp(32, ("arbitrary",)),
        name="router",
    )(x, mods, mods, wr_t, br_col)


def _slots_kernel(mk_ref, cw_ref, start_ref, before_ref, lower_ref, pos_ref, w_ref, carry_ref):
    i = pl.program_id(0)

    @pl.when(i == 0)
    def _():
        carry_ref[...] = jnp.zeros_like(carry_ref)

    mk = mk_ref[...]
    rank = jnp.dot(mk, before_ref[...], preferred_element_type=F32)
    pos = start_ref[...] + carry_ref[...] + rank
    order = jnp.dot(lower_ref[...], mk, preferred_element_type=F32)
    chosen = mk > 0
    cw = cw_ref[...]
    rows_p, rows_w = [], []
    for k in range(TOP_K):
        pick = jnp.logical_and(chosen, order == k)
        rows_p.append(jnp.sum(jnp.where(pick, pos, 0.0), axis=0, keepdims=True))
        rows_w.append(jnp.sum(jnp.where(pick, cw, 0.0), axis=0, keepdims=True))
    pos_ref[...] = jnp.concatenate(rows_p, axis=0).astype(I32)
    w_ref[...] = jnp.concatenate(rows_w, axis=0)
    carry_ref[...] += jnp.sum(mk.astype(F32), axis=1, keepdims=True)


def _slots(mk, cw, start_col):
    ne, t = mk.shape
    tm = 256
    r = jnp.arange(tm)
    before = (r[:, None] < r[None, :]).astype(BF16)
    return pl.pallas_call(
        _slots_kernel,
        out_shape=(jax.ShapeDtypeStruct((TOP_K, t), I32), jax.ShapeDtypeStruct((TOP_K, t), F32)),
        grid=(t // tm,),
        in_specs=[
            pl.BlockSpec((ne, tm), lambda i: (0, i)),
            pl.BlockSpec((ne, tm), lambda i: (0, i)),
            pl.BlockSpec((ne, 1), lambda i: (0, 0)),
            pl.BlockSpec((tm, tm), lambda i: (0, 0)),
            pl.BlockSpec((ne, ne), lambda i: (0, 0)),
        ],
        out_specs=(pl.BlockSpec((TOP_K, tm), lambda i: (0, i)), pl.BlockSpec((TOP_K, tm), lambda i: (0, i))),
        scratch_shapes=[pltpu.VMEM((ne, 1), F32)],
        compiler_params=_cp(32, ("arbitrary",)),
        name="plan",
    )(mk, cw, start_col, before, _strict_lower(ne))


def _dispatch_kernel(fill_ref, pad_ref, pos_ref, xm_ref, xs_ref, zeros_ref, sem):
    i = pl.program_id(0)
    tm = xm_ref.shape[0] // TOKEN_ROWS
    ne = fill_ref.shape[0]
    pad_bits = (zeros_ref.shape[0] // TOKEN_ROWS).bit_length() - 1

    def row_copy(r, k):
        return pltpu.make_async_copy(_token_tile(xm_ref, r), _token_tile(xs_ref, pos_ref[r * TOP_K + k]), sem)

    def fill(e, wait):
        first, n = fill_ref[e], pad_ref[e]
        for bit in range(pad_bits):
            size = (1 << bit) * TOKEN_ROWS
            done = (n >> (bit + 1)) << (bit + 1)
            dst = pl.multiple_of((first + done) * TOKEN_ROWS, TOKEN_ROWS)
            piece = pltpu.make_async_copy(zeros_ref.at[pl.ds(0, size)], xs_ref.at[pl.ds(dst, size)], sem)

            @pl.when(((n >> bit) & 1) == 1)
            def _(piece=piece):
                if wait:
                    piece.wait()
                else:
                    piece.start()

    @pl.when(i == 0)
    def _():
        zeros_ref[...] = jnp.zeros_like(zeros_ref)

        def start(e, c):
            fill(e, False)
            return c

        def wait(e, c):
            fill(e, True)
            return c

        lax.fori_loop(0, ne, start, 0)
        lax.fori_loop(0, ne, wait, 0)

    def start(r, c):
        for k in range(TOP_K):
            row_copy(r, k).start(priority=k % 2)
        return c

    def wait(r, c):
        for k in range(TOP_K):
            row_copy(r, k).wait()
        return c

    lax.fori_loop(0, tm, start, 0)
    lax.fori_loop(0, tm, wait, 0)


def _dispatch(fill, pad, pos_flat, xm, n_slots):
    t = xm.shape[0] // TOKEN_ROWS
    tm = 256
    return pl.pallas_call(
        _dispatch_kernel,
        out_shape=jax.ShapeDtypeStruct((n_slots * TOKEN_ROWS, LANES), U32),
        grid_spec=pltpu.PrefetchScalarGridSpec(
            num_scalar_prefetch=2,
            grid=(t // tm,),
            in_specs=[
                pl.BlockSpec((tm * TOP_K,), lambda i, fill, pad: (i,), memory_space=pltpu.SMEM),
                pl.BlockSpec((tm * TOKEN_ROWS, LANES), lambda i, fill, pad: (i, 0)),
            ],
            out_specs=pl.BlockSpec(memory_space=pl.ANY),
            scratch_shapes=[pltpu.VMEM((EXPERT_TILE * TOKEN_ROWS, LANES), U32), pltpu.SemaphoreType.DMA(())],
        ),
        compiler_params=_cp(32, ("arbitrary",)),
        name="dispatch",
    )(fill, pad, pos_flat, xm)


def _gmm_kernel(te_ref, nt_ref, nxt_ref, slot_ref, xs_ref, wg_ref, wu_ref, wd_ref, ys_ref,
                fg_ref, fu_ref, fd_ref, wgb_ref, wub_ref, wdb_ref, sems, *, layer):
    i = pl.program_id(0)
    e = te_ref[i]
    s = slot_ref[i]
    prev = te_ref[jnp.maximum(i - 1, 0)]

    def fetch(expert, slot):
        return (pltpu.make_async_copy(wg_ref.at[layer, expert], fg_ref.at[slot], sems.at[slot]),
                pltpu.make_async_copy(wu_ref.at[layer, expert], fu_ref.at[slot], sems.at[slot]),
                pltpu.make_async_copy(wd_ref.at[layer, expert], fd_ref.at[slot], sems.at[slot]))

    @pl.when(i == 0)
    def _():
        for c in fetch(e, s):
            c.start()

    for slot in range(2):
        @pl.when(jnp.logical_and(jnp.logical_or(i == 0, e != prev), s == slot))
        def _(slot=slot):
            for c in fetch(e, slot):
                c.wait()
            wgb_ref[...] = fg_ref[slot].astype(BF16)
            wub_ref[...] = fu_ref[slot].astype(BF16)
            wdb_ref[...] = fd_ref[slot].astype(BF16)

            @pl.when(nxt_ref[i] >= 0)
            def _():
                for c in fetch(nxt_ref[i], 1 - slot):
                    c.start()

    @pl.when(i < nt_ref[0])
    def _():
        lo, hi = _unpack_halves(_load_token_tiles(xs_ref))
        x = jnp.concatenate([lo, hi], axis=1).astype(BF16)
        hg = jnp.dot(x, wgb_ref[...], preferred_element_type=F32)
        hu = jnp.dot(x, wub_ref[...], preferred_element_type=F32)
        hid = (jax.nn.silu(hg) * hu).astype(BF16)
        y = jnp.dot(hid, wdb_ref[...], preferred_element_type=F32)
        _store_token_tiles(ys_ref, _pack_halves(y))


def _gmm(tile_expert, n_tiles_used, tile_next, tile_slot, xs, wg, wu, wd, layer, max_tiles):
    d, f = wg.shape[2], wg.shape[3]
    tm = EXPERT_TILE
    row = lambda i, te, nt, nx, sl: (jnp.minimum(i, nt[0] - 1), 0)
    return pl.pallas_call(
        functools.partial(_gmm_kernel, layer=layer),
        out_shape=jax.ShapeDtypeStruct(xs.shape, U32),
        grid_spec=pltpu.PrefetchScalarGridSpec(
            num_scalar_prefetch=4,
            grid=(max_tiles,),
            in_specs=[
                pl.BlockSpec((tm * TOKEN_ROWS, LANES), row),
                pl.BlockSpec(memory_space=pl.ANY),
                pl.BlockSpec(memory_space=pl.ANY),
                pl.BlockSpec(memory_space=pl.ANY),
            ],
            out_specs=pl.BlockSpec((tm * TOKEN_ROWS, LANES), row),
            scratch_shapes=[pltpu.VMEM((2, d, f), F32), pltpu.VMEM((2, d, f), F32), pltpu.VMEM((2, f, d), F32),
                            pltpu.VMEM((d, f), BF16), pltpu.VMEM((d, f), BF16), pltpu.VMEM((f, d), BF16),
                            pltpu.SemaphoreType.DMA((2,))],
        ),
        compiler_params=_cp(56, ("arbitrary",)),
        name="gmm",
    )(tile_expert, n_tiles_used, tile_next, tile_slot, xs, wg, wu, wd)


def _combine_kernel(pos_ref, nxt_ref, w_ref, xm_ref, x_ref, gate_ref, wsg_ref, wsu_ref, wsd_ref,
                    g_ref, b_ref, ys_ref, o_ref, buf_ref, sems):
    i = pl.program_id(0)
    n = pl.num_programs(0)
    tm, d = x_ref.shape
    half = d // 2
    slot = i % 2

    def copy(p_ref, s, r, k):
        return pltpu.make_async_copy(_token_tile(ys_ref, p_ref[r * TOP_K + k]),
                                     _token_tile(buf_ref.at[s, k], r), sems.at[s])

    def gather(p_ref, s):
        def start(r, c):
            for k in range(TOP_K):
                copy(p_ref, s, r, k).start(priority=k % 2)
            return c
        lax.fori_loop(0, tm, start, 0)

    @pl.when(i == 0)
    def _():
        gather(pos_ref, 0)

    for s in range(2):
        @pl.when(jnp.logical_and(i + 1 < n, slot == 1 - s))
        def _(s=s):
            gather(nxt_ref, s)

    lo, hi = _unpack_halves(_load_token_tiles(xm_ref))
    xb = jnp.concatenate([lo, hi], axis=1).astype(BF16)
    hg = jnp.dot(xb, wsg_ref[...], preferred_element_type=F32)
    hu = jnp.dot(xb, wsu_ref[...], preferred_element_type=F32)
    f = jnp.dot((jax.nn.silu(hg) * hu).astype(BF16), wsd_ref[...], preferred_element_type=F32)

    def wait(r, c):
        for k in range(TOP_K):
            copy(pos_ref, slot, r, k).wait()
        return c

    lax.fori_loop(0, tm, wait, 0)
    w = w_ref[...]
    wk = [jnp.broadcast_to(w[:, k:k + 1], (tm, LANES)) for k in range(TOP_K)]
    los, his = [], []
    for c in range(TOKEN_ROWS):
        acc_lo = f[:, c * LANES:(c + 1) * LANES]
        acc_hi = f[:, half + c * LANES:half + (c + 1) * LANES]
        for k in range(TOP_K):
            lo, hi = _unpack_halves(buf_ref[slot, k, pl.ds(c, tm, stride=TOKEN_ROWS), :])
            acc_lo = acc_lo + wk[k] * lo
            acc_hi = acc_hi + wk[k] * hi
        los.append(acc_lo)
        his.append(acc_hi)
    f = jnp.concatenate(los + his, axis=1)
    o_ref[...] = _residual_ln(x_ref[...], f, gate_ref, g_ref, b_ref)


def _combine(pos_flat, w, xm, x, mods, layer, wsg, wsu, wsd, g, b, ys, row0, n_rows):
    d = x.shape[1]
    fs = wsg.shape[2]
    tm = 128
    ng = tm // GROUP
    off = row0 // tm
    last = n_rows // tm - 1
    return pl.pallas_call(
        _combine_kernel,
        out_shape=jax.ShapeDtypeStruct((n_rows, d), F32),
        grid=(n_rows // tm,),
        in_specs=[
            pl.BlockSpec((tm * TOP_K,), lambda i: (i + off,), memory_space=pltpu.SMEM),
            pl.BlockSpec((tm * TOP_K,), lambda i: (jnp.minimum(i + 1, last) + off,), memory_space=pltpu.SMEM),
            pl.BlockSpec((tm, TOP_K), lambda i: (i + off, 0)),
            pl.BlockSpec((tm * TOKEN_ROWS, LANES), lambda i: (i + off, 0)),
            pl.BlockSpec((tm, d), lambda i: (i + off, 0)),
            pl.BlockSpec((1, ng, 1, d), lambda i: (layer, i + off, 0, 2)),
            pl.BlockSpec((None, d, fs), lambda i: (layer, 0, 0)),
            pl.BlockSpec((None, d, fs), lambda i: (layer, 0, 0)),
            pl.BlockSpec((None, fs, d), lambda i: (layer, 0, 0)),
            pl.BlockSpec((1, d), lambda i: (0, 0)),
            pl.BlockSpec((1, d), lambda i: (0, 0)),
            pl.BlockSpec(memory_space=pl.ANY),
        ],
        out_specs=pl.BlockSpec((tm, d), lambda i: (i, 0)),
        scratch_shapes=[pltpu.VMEM((2, TOP_K, tm * TOKEN_ROWS, LANES), U32), pltpu.SemaphoreType.DMA((2,))],
        compiler_params=_cp(48, ("arbitrary",)),
        name="combine",
    )(pos_flat, pos_flat, w, xm, x, mods, wsg, wsu, wsd, g, b, ys)


def _moe(x, mods, layer, wr, br, wg, wu, wd, wsg, wsu, wsd, g, b, splits):
    t = x.shape[0]
    ne = wr.shape[1]
    tile = EXPERT_TILE
    max_tiles = (t * TOP_K) // tile + ne
    xm, cw, mk, cnt = _route(x, mods, layer, wr.T, br.reshape(ne, 1))
    ids = jnp.arange(ne, dtype=I32)
    upto = ids[None, :] <= ids[:, None]
    cnt = cnt[:, 0].astype(I32)
    tiles = (cnt + tile - 1) // tile
    ends = jnp.sum(jnp.where(upto, tiles[None, :], 0), axis=1)
    start = (ends - tiles) * tile
    n_used = ends[-1:]
    tile_ids = jnp.minimum(jnp.arange(max_tiles, dtype=I32), n_used[0] - 1)
    tile_expert = jnp.sum((ends[None, :] <= tile_ids[:, None]).astype(I32), axis=1)
    used = tiles > 0
    later_used = jnp.logical_and(ids[None, :] > ids[:, None], used[None, :])
    next_used = jnp.min(jnp.where(later_used, ids[None, :], ne), axis=1)
    next_used = jnp.where(next_used < ne, next_used, -1).astype(I32)
    slot = ((jnp.sum(jnp.logical_and(upto, used[None, :]).astype(I32), axis=1) - 1) % 2).astype(I32)
    pos, w = _slots(mk, cw, start.astype(F32)[:, None])
    pos_flat = pos.T.reshape(-1)
    w = w.T
    xs = _dispatch(start + cnt, tiles * tile - cnt, pos_flat, xm, max_tiles * tile)
    of_tile = (tile_expert[:, None] == ids[None, :]).astype(I32)
    tile_next = jnp.sum(of_tile * next_used[None, :], axis=1)
    tile_slot = jnp.sum(of_tile * slot[None, :], axis=1)
    ys = _gmm(tile_expert, n_used.astype(I32), tile_next, tile_slot, xs, wg, wu, wd, layer, max_tiles)
    return [_combine(pos_flat, w, xm, x, mods, layer, wsg, wsu, wsd, g, b, ys, row0, n_rows)
            for row0, n_rows in splits]


def _expand_mods(mod, n_prompt_groups, n_streams):
    depth, _, n = mod.shape
    p = jnp.broadcast_to(mod[:, 0:1], (depth, n_prompt_groups, n))
    return jnp.concatenate([p, mod[:, 1:1 + n_streams]], axis=1)[:, :, None, :]


def kernel(x_prompt, x_sample, c_prompt, c_sample, cache_k, cache_v, w_ada_mix, b_ada_mix, w_ada_ffn, b_ada_ffn, ln_mix_g, ln_mix_b, ln_ffn_g, ln_ffn_b, w_qkv, w_o, w_uv, b_uv, ln_v_g, ln_v_b, w_s, b_s, w_cm_out, w_router, b_router, w_gate, w_up, w_down, ws_gate, ws_up, ws_down):
    bp, sp, d = x_prompt.shape
    bs, t_new, _ = x_sample.shape
    assert bp == 1 and t_new == GROUP and sp % ATT_BLOCK == 0 and d == N_HEADS * HEAD_DIM
    n_prompt = bp * sp
    n_sample = bs * t_new
    width = w_uv.shape[2] // 2
    past = cache_k.shape[2]

    x = jnp.concatenate([x_prompt.reshape(n_prompt, d), x_sample.reshape(n_sample, d)], axis=0)

    rows = 8 * ((1 + bs + 7) // 8)
    c_all = jnp.zeros((rows, d), F32).at[0:1].set(c_prompt).at[1:1 + bs].set(c_sample)
    mods_mix = _expand_mods(_ada(c_all, w_ada_mix, b_ada_mix), n_prompt // GROUP, bs)
    mods_ffn = _expand_mods(_ada(c_all, w_ada_ffn, b_ada_ffn), n_prompt // GROUP, bs)

    cpos = jnp.arange(CM_CHUNK)
    cmask = (cpos[None, :] // CHUNK) <= (cpos[:, None] // CHUNK)
    pair = (cpos[None, :] // t_new) == (cpos[:, None] // t_new)
    fold = cpos % t_new
    smask = (fold[None, :] // CHUNK) <= (fold[:, None] // CHUNK)
    ws_prompt = jnp.where(cmask, w_s, 0.0)
    ws_sample = jnp.where(jnp.logical_and(pair, smask), w_s[:, :, fold][:, :, :, fold], 0.0)
    mix = jnp.stack([ws_prompt, ws_sample], axis=1).astype(BF16)
    gw = width // CM_GROUPS
    bias_p = jnp.repeat(jnp.swapaxes(b_s, 1, 2), gw, axis=2)
    mix_bias = jnp.stack([bias_p, bias_p[:, fold]], axis=1)

    ck = cache_k.reshape(cache_k.shape[0], bs, past * N_HEADS, HEAD_DIM)
    cv = cache_v.reshape(cache_v.shape[0], bs, past * N_HEADS, HEAD_DIM)

    n_sb = (DEPTH + 1) // 2
    n_cm = DEPTH // 2
    kp = vp = ksm = vsm = cms = cmp_ = None
    wq, wo = w_qkv.astype(BF16), w_o.astype(BF16)
    wuv, wcm = w_uv.astype(BF16), w_cm_out.astype(BF16)
    wsg, wsu, wsd = ws_gate.astype(BF16), ws_up.astype(BF16), ws_down.astype(BF16)
    buv = b_uv[:, None, :]
    for i in range(DEPTH):
        j = i // 2
        if i % 2 == 0:
            qkv_p, kp, vp = _qkv(x, mods_mix, i, wq, 0, n_prompt, j, n_sb, kp, vp)
            qkv_s, ksm, vsm = _qkv(x, mods_mix, i, wq, n_prompt, n_sample, j, n_sb, ksm, vsm)
            o_p = _attn_prompt(qkv_p, n_prompt)
            o_s, cs = _attn_sample(qkv_s, ck, cv, j, 0, bs, t_new)
            o_s = lax.cond(
                jnp.max(cs) > STICK_DONE,
                lambda: _attn_sample_rest(qkv_s, o_s, cs, ck, cv, j, 0, bs, t_new),
                lambda: o_s)
            x = _proj_ln(o_p, o_s, x, mods_mix, i, wo, j, ln_mix_g[i][None], ln_mix_b[i][None])
        else:
            zz = _modmm(x, mods_mix, i, wuv, buv, j, BF16, True)
            x, cms, cmp_ = _cm_mix(zz, x, mods_mix, i, mix[j], mix_bias[j], ln_v_g[j][None], ln_v_b[j][None],
                                   wcm, ln_mix_g[i][None], ln_mix_b[i][None],
                                   n_prompt, j, n_cm, cms, cmp_)
        last = i == DEPTH - 1
        splits = [(0, n_prompt), (n_prompt, n_sample)] if last else [(0, n_prompt + n_sample)]
        outs = _moe(x, mods_ffn, i, w_router[i], b_router[i][None], w_gate, w_up, w_down,
                    wsg, wsu, wsd, ln_ffn_g[i][None], ln_ffn_b[i][None], splits)
        x = outs[0]

    y_prompt, y_sample = outs
    return (
        y_prompt.reshape(bp, sp, d),
        y_sample.reshape(bs, t_new, d),
        kp.reshape(n_sb, bp, sp, N_HEADS, HEAD_DIM),
        vp.reshape(n_sb, bp, sp, N_HEADS, HEAD_DIM),
        ksm.reshape(n_sb, bs, t_new, N_HEADS, HEAD_DIM),
        vsm.reshape(n_sb, bs, t_new, N_HEADS, HEAD_DIM),
        cmp_[:, :CM_CHUNK].reshape(n_cm, bp, CM_CHUNK, width),
        cms.reshape(n_cm, bs, t_new, width),
    )
```

```python
import functools

import jax
import jax.numpy as jnp
from jax import lax
from jax.experimental import pallas as pl
from jax.experimental.pallas import tpu as pltpu

F32 = jnp.float32
BF16 = jnp.bfloat16
I32 = jnp.int32

DEPTH = 4
N_HEADS = 16
HEAD_DIM = 128
CHUNK = 64
CM_CHUNK = 128
CM_GROUPS = 16
N_EXPERTS = 64
N_EXPERT_GROUPS = 8
TOPK_GROUPS = 4
TOP_K = 8
ROUTED_SCALE = 2.5
ALPHA = (2 * DEPTH) ** 0.25
LN_EPS = 1e-5

GROUP = 64
MIB = 1024 * 1024
STICK_DONE = -104.0
ATT_BLOCK = 256
EXPERT_TILE = 256


def _cp(vmem_mib, sem):
    return pltpu.CompilerParams(dimension_semantics=sem, vmem_limit_bytes=vmem_mib * MIB)


def _ln(y, g, b):
    mu = jnp.mean(y, axis=-1, keepdims=True)
    yc = y - mu
    var = jnp.mean(yc * yc, axis=-1, keepdims=True)
    return yc * lax.rsqrt(var + LN_EPS) * g + b


def _modulate(x, sh_ref, sc_ref):
    tm, d = x.shape
    g = sh_ref.shape[1]
    xg = x.reshape(g, tm // g, d)
    return (xg * sc_ref[0] + sh_ref[0]).reshape(tm, d)


def _residual_ln(x, f, gate_ref, g_ref, b_ref):
    tm, d = x.shape
    g = gate_ref.shape[1]
    y = ALPHA * x + (gate_ref[0] * f.reshape(g, tm // g, d)).reshape(tm, d)
    return _ln(y, g_ref[...], b_ref[...])


TOKEN_ROWS = 8
LANES = 128
U32 = jnp.uint32


def _pack_halves(y):
    n = y.shape[1] // 2
    lo = lax.bitcast_convert_type(y[:, :n].astype(BF16).astype(F32), U32)
    hi = lax.bitcast_convert_type(y[:, n:].astype(BF16).astype(F32), U32)
    return hi | (lo >> 16)


def _unpack_halves(w):
    lo = lax.bitcast_convert_type(w << 16, F32)
    hi = lax.bitcast_convert_type(w & jnp.uint32(0xFFFF0000), F32)
    return lo, hi


def _store_token_tiles(ref, words):
    tm = words.shape[0]
    for c in range(TOKEN_ROWS):
        ref[pl.ds(c, tm, stride=TOKEN_ROWS), :] = words[:, c * LANES:(c + 1) * LANES]


def _load_token_tiles(ref):
    tm = ref.shape[0] // TOKEN_ROWS
    return jnp.concatenate([ref[pl.ds(c, tm, stride=TOKEN_ROWS), :] for c in range(TOKEN_ROWS)], axis=1)


def _token_tile(ref, i):
    return ref.at[pl.ds(pl.multiple_of(i * TOKEN_ROWS, TOKEN_ROWS), TOKEN_ROWS)]


def _ada_kernel(c_ref, w_ref, b_ref, o_ref, *, d_model, tn):
    j = pl.program_id(1)
    a = jax.nn.silu(c_ref[...]).astype(BF16)
    acc = jnp.dot(a, w_ref[0].astype(BF16), preferred_element_type=F32) + b_ref[0]
    o_ref[0] = acc + jnp.where(j * tn >= d_model, 1.0, 0.0).astype(F32)


def _ada(c_all, w, b):
    depth, d, n = w.shape
    r = c_all.shape[0]
    tn = 1024
    return pl.pallas_call(
        functools.partial(_ada_kernel, d_model=d, tn=tn),
        out_shape=jax.ShapeDtypeStruct((depth, r, n), F32),
        grid=(depth, n // tn),
        in_specs=[
            pl.BlockSpec((r, d), lambda l, j: (0, 0)),
            pl.BlockSpec((1, d, tn), lambda l, j: (l, 0, j)),
            pl.BlockSpec((1, 1, tn), lambda l, j: (l, 0, j)),
        ],
        out_specs=pl.BlockSpec((1, r, tn), lambda l, j: (l, 0, j)),
        compiler_params=_cp(40, ("parallel", "parallel")),
        name="ada",
    )(c_all, w, b.reshape(depth, 1, n))


def _modmm_kernel(x_ref, sh_ref, sc_ref, w_ref, b_ref, o_ref, *, gelu):
    h = _modulate(x_ref[...], sh_ref, sc_ref).astype(BF16)
    acc = jnp.dot(h, w_ref[...], preferred_element_type=F32) + b_ref[...]
    if gelu:
        acc = jax.nn.gelu(acc)
    o_ref[...] = acc.astype(o_ref.dtype)


def _modmm(x, mods, layer, w, b, wslot, out_dtype, gelu):
    t, d = x.shape
    n = w.shape[2]
    tm, tn = 512, 2048
    g = tm // GROUP
    return pl.pallas_call(
        functools.partial(_modmm_kernel, gelu=gelu),
        out_shape=jax.ShapeDtypeStruct((t, n), out_dtype),
        grid=(n // tn, t // tm),
        in_specs=[
            pl.BlockSpec((tm, d), lambda j, i: (i, 0)),
            pl.BlockSpec((1, g, 1, d), lambda j, i: (layer, i, 0, 0)),
            pl.BlockSpec((1, g, 1, d), lambda j, i: (layer, i, 0, 1)),
            pl.BlockSpec((None, d, tn), lambda j, i: (wslot, 0, j)),
            pl.BlockSpec((None, 1, tn), lambda j, i: (wslot, 0, j)),
        ],
        out_specs=pl.BlockSpec((tm, tn), lambda j, i: (i, j)),
        compiler_params=_cp(48, ("parallel", "parallel")),
        name="modmm",
    )(x, mods, mods, w, b)


def _qkv_kernel(x_ref, sh_ref, sc_ref, w_ref, *rest):
    qkv_ref, kf_ref, vf_ref = rest[-3:]
    h = _modulate(x_ref[...], sh_ref, sc_ref).astype(BF16)
    acc = jnp.dot(h, w_ref[...], preferred_element_type=F32)
    qkv_ref[...] = acc.astype(BF16)
    tm = acc.shape[0]
    d = acc.shape[1] // 3
    for hd in range(N_HEADS):
        cols = slice(hd * HEAD_DIM, (hd + 1) * HEAD_DIM)
        kf_ref[0, pl.ds(hd, tm, stride=N_HEADS), :] = acc[:, d:2 * d][:, cols]
        vf_ref[0, pl.ds(hd, tm, stride=N_HEADS), :] = acc[:, 2 * d:][:, cols]


def _qkv(x, mods, layer, w, row0, n_rows, slot, n_slots, kf_prev, vf_prev):
    d = x.shape[1]
    tm = 256
    g = tm // GROUP
    off = row0 // tm
    flat = (n_slots, n_rows * N_HEADS, HEAD_DIM)
    in_specs = [
        pl.BlockSpec((tm, d), lambda i: (i + off, 0)),
        pl.BlockSpec((1, g, 1, d), lambda i: (layer, i + off, 0, 0)),
        pl.BlockSpec((1, g, 1, d), lambda i: (layer, i + off, 0, 1)),
        pl.BlockSpec((None, d, 3 * d), lambda i: (slot, 0, 0), pipeline_mode=pl.Buffered(1)),
    ]
    args = [x, mods, mods, w]
    aliases = {}
    if kf_prev is not None:
        in_specs += [pl.BlockSpec(memory_space=pl.ANY), pl.BlockSpec(memory_space=pl.ANY)]
        args += [kf_prev, vf_prev]
        aliases = {4: 1, 5: 2}
    return pl.pallas_call(
        _qkv_kernel,
        out_shape=(jax.ShapeDtypeStruct((n_rows, 3 * d), BF16),
                   jax.ShapeDtypeStruct(flat, F32), jax.ShapeDtypeStruct(flat, F32)),
        grid=(n_rows // tm,),
        in_specs=in_specs,
        out_specs=(pl.BlockSpec((tm, 3 * d), lambda i: (i, 0)),
                   pl.BlockSpec((1, tm * N_HEADS, HEAD_DIM), lambda i: (slot, i, 0)),
                   pl.BlockSpec((1, tm * N_HEADS, HEAD_DIM), lambda i: (slot, i, 0))),
        input_output_aliases=aliases,
        compiler_params=_cp(56, ("parallel",)),
        name="qkv",
    )(*args)


def _sb_block(q, k, v, csum, u, mask):
    z = lax.dot_general(q, k, (((1,), (1,)), ((), ())), preferred_element_type=F32)
    z = z * (HEAD_DIM ** -0.5)
    t = jnp.log(1.0 + jnp.exp(-jnp.abs(z)))
    log_beta = -(jnp.maximum(-z, 0.0) + t)
    l1 = log_beta - z
    if mask is not None:
        l1 = jnp.where(mask, l1, 0.0)
    hi = l1.astype(BF16)
    lo = (l1 - hi.astype(F32)).astype(BF16)
    s = jnp.dot(hi, u, preferred_element_type=F32) + jnp.dot(lo, u, preferred_element_type=F32)
    a = jnp.exp(log_beta + s + csum)
    if mask is not None:
        a = jnp.where(mask, a, 0.0)
    o = jnp.dot(a.astype(BF16), v, preferred_element_type=F32)
    return o, csum + s[:, :1] + l1[:, :1]


def _causal_mask(n):
    row = lax.broadcasted_iota(I32, (n, n), 0)
    col = lax.broadcasted_iota(I32, (n, n), 1)
    return col < row


def _attn_prompt_kernel(q_ref, k_ref, v_ref, u_ref, o_ref):
    i = pl.program_id(1)
    bq = q_ref.shape[0]
    heads = q_ref.shape[1] // HEAD_DIM
    u = u_ref[...]
    mask = _causal_mask(bq)

    def kv(b, cols):
        start = pl.multiple_of(b * bq, bq)
        return (k_ref[pl.ds(start, bq), cols].astype(BF16), v_ref[pl.ds(start, bq), cols].astype(BF16))

    prev = jnp.maximum(i - 1, 0)
    has_prev = i > 0
    qs, os, css = [], [], []
    for h in range(heads):
        cols = slice(h * HEAD_DIM, (h + 1) * HEAD_DIM)
        q = q_ref[:, cols].astype(BF16)
        k0, v0 = kv(i, cols)
        o, cs = _sb_block(q, k0, v0, jnp.zeros((bq, 1), F32), u, mask)
        k1, v1 = kv(prev, cols)
        do, cs1 = _sb_block(q, k1, v1, cs, u, None)
        qs.append(q)
        os.append(o + jnp.where(has_prev, do, 0.0))
        css.append(jnp.where(has_prev, cs1, cs))

    def cond(c):
        b, _, css = c
        live = jnp.max(css[0])
        for cs in css[1:]:
            live = jnp.maximum(live, jnp.max(cs))
        return jnp.logical_and(b >= 0, live > STICK_DONE)

    def body(c):
        b, os, css = c
        new_os, new_css = [], []
        for h in range(heads):
            cols = slice(h * HEAD_DIM, (h + 1) * HEAD_DIM)
            kb, vb = kv(b, cols)
            do, cs = _sb_block(qs[h], kb, vb, css[h], u, None)
            new_os.append(os[h] + do)
            new_css.append(cs)
        return b - 1, tuple(new_os), tuple(new_css)

    _, os, _ = lax.while_loop(cond, body, (i - 2, tuple(os), tuple(css)))
    for h in range(heads):
        o_ref[:, h * HEAD_DIM:(h + 1) * HEAD_DIM] = os[h].astype(o_ref.dtype)


def _strict_lower(n):
    r = jnp.arange(n)
    return (r[:, None] > r[None, :]).astype(BF16)


def _attn_prompt(qkv, n_prompt):
    d = N_HEADS * HEAD_DIM
    bq = ATT_BLOCK
    hp = 4
    groups = N_HEADS // hp
    wcols = hp * HEAD_DIM
    return pl.pallas_call(
        _attn_prompt_kernel,
        out_shape=jax.ShapeDtypeStruct((n_prompt, d), BF16),
        grid=(groups, n_prompt // bq),
        in_specs=[
            pl.BlockSpec((bq, wcols), lambda h, i: (i, h)),
            pl.BlockSpec((n_prompt, wcols), lambda h, i: (0, groups + h)),
            pl.BlockSpec((n_prompt, wcols), lambda h, i: (0, 2 * groups + h)),
            pl.BlockSpec((bq, bq), lambda h, i: (0, 0)),
        ],
        out_specs=pl.BlockSpec((bq, wcols), lambda h, i: (i, h)),
        compiler_params=_cp(48, ("parallel", "parallel")),
        name="attn_prompt",
    )(qkv, qkv, qkv, _strict_lower(bq))


def _attn_sample_kernel(q_ref, kn_ref, vn_ref, ck_ref, cv_ref, un_ref, uc_ref, o_ref, cs_ref):
    tq = q_ref.shape[0]
    bk = uc_ref.shape[0]
    mask = _causal_mask(tq)
    for h in range(N_HEADS):
        cols = slice(h * HEAD_DIM, (h + 1) * HEAD_DIM)
        q = q_ref[:, cols].astype(BF16)
        o, cs = _sb_block(q, kn_ref[:, cols].astype(BF16), vn_ref[:, cols].astype(BF16),
                          jnp.zeros((tq, 1), F32), un_ref[...], mask)
        kc = ck_ref[0, 0, pl.ds(h, bk, stride=N_HEADS), :].astype(BF16)
        vc = cv_ref[0, 0, pl.ds(h, bk, stride=N_HEADS), :].astype(BF16)
        do, cs = _sb_block(q, kc, vc, cs, uc_ref[...], None)
        o_ref[:, cols] = o + do
        cs_ref[:, cols] = jnp.broadcast_to(cs, (tq, HEAD_DIM))


def _attn_sample(qkv, cache_k, cache_v, layer, n_prompt, n_streams, t_new):
    d = N_HEADS * HEAD_DIM
    past = cache_k.shape[2] // N_HEADS
    bk = ATT_BLOCK
    pb = n_prompt // t_new
    last = past // bk - 1
    return pl.pallas_call(
        _attn_sample_kernel,
        out_shape=(jax.ShapeDtypeStruct((n_streams * t_new, d), F32),
                   jax.ShapeDtypeStruct((n_streams * t_new, d), F32)),
        grid=(n_streams,),
        in_specs=[
            pl.BlockSpec((t_new, d), lambda s: (pb + s, 0)),
            pl.BlockSpec((t_new, d), lambda s: (pb + s, 1)),
            pl.BlockSpec((t_new, d), lambda s: (pb + s, 2)),
            pl.BlockSpec((1, 1, bk * N_HEADS, HEAD_DIM), lambda s: (layer, s, last, 0)),
            pl.BlockSpec((1, 1, bk * N_HEADS, HEAD_DIM), lambda s: (layer, s, last, 0)),
            pl.BlockSpec((t_new, t_new), lambda s: (0, 0)),
            pl.BlockSpec((bk, bk), lambda s: (0, 0)),
        ],
        out_specs=(pl.BlockSpec((t_new, d), lambda s: (s, 0)),
                   pl.BlockSpec((t_new, d), lambda s: (s, 0))),
        compiler_params=_cp(32, ("parallel",)),
        name="attn_sample",
    )(qkv, qkv, qkv, cache_k, cache_v, _strict_lower(t_new), _strict_lower(bk))


def _attn_sample_rest_kernel(q_ref, oin_ref, csin_ref, ck_ref, cv_ref, u_ref, o_ref, cs_ref):
    b = pl.program_id(1)
    bk = u_ref.shape[0]

    @pl.when(b == 0)
    def _():
        o_ref[...] = oin_ref[...]
        cs_ref[...] = csin_ref[...]

    for h in range(N_HEADS):
        cols = slice(h * HEAD_DIM, (h + 1) * HEAD_DIM)
        cs = cs_ref[:, cols][:, :1]

        @pl.when(jnp.max(cs) > STICK_DONE)
        def _(h=h, cols=cols, cs=cs):
            q = q_ref[:, cols].astype(BF16)
            kc = ck_ref[0, 0, pl.ds(h, bk, stride=N_HEADS), :].astype(BF16)
            vc = cv_ref[0, 0, pl.ds(h, bk, stride=N_HEADS), :].astype(BF16)
            do, cs2 = _sb_block(q, kc, vc, cs, u_ref[...], None)
            o_ref[:, cols] += do
            cs_ref[:, cols] = jnp.broadcast_to(cs2, (cs2.shape[0], HEAD_DIM))


def _attn_sample_rest(qkv, o_part, cs_part, cache_k, cache_v, layer, n_prompt, n_streams, t_new):
    d = N_HEADS * HEAD_DIM
    past = cache_k.shape[2] // N_HEADS
    bk = ATT_BLOCK
    pb = n_prompt // t_new
    nb = past // bk - 1
    o, _ = pl.pallas_call(
        _attn_sample_rest_kernel,
        out_shape=(jax.ShapeDtypeStruct((n_streams * t_new, d), F32),
                   jax.ShapeDtypeStruct((n_streams * t_new, d), F32)),
        grid=(n_streams, nb),
        in_specs=[
            pl.BlockSpec((t_new, d), lambda s, b: (pb + s, 0)),
            pl.BlockSpec((t_new, d), lambda s, b: (s, 0)),
            pl.BlockSpec((t_new, d), lambda s, b: (s, 0)),
            pl.BlockSpec((1, 1, bk * N_HEADS, HEAD_DIM), lambda s, b: (layer, s, nb - 1 - b, 0)),
            pl.BlockSpec((1, 1, bk * N_HEADS, HEAD_DIM), lambda s, b: (layer, s, nb - 1 - b, 0)),
            pl.BlockSpec((bk, bk), lambda s, b: (0, 0)),
        ],
        out_specs=(pl.BlockSpec((t_new, d), lambda s, b: (s, 0)),
                   pl.BlockSpec((t_new, d), lambda s, b: (s, 0))),
        compiler_params=_cp(32, ("parallel", "arbitrary")),
        name="attn_sample_rest",
    )(qkv, o_part, cs_part, cache_k, cache_v, _strict_lower(bk))
    return o


def _proj_ln_kernel(ap_ref, as_ref, x_ref, gate_ref, w_ref, g_ref, b_ref, o_ref, *, prompt_tiles):
    a = jnp.where(pl.program_id(0) < prompt_tiles, ap_ref[...], as_ref[...].astype(BF16))
    f = jnp.dot(a, w_ref[...], preferred_element_type=F32)
    o_ref[...] = _residual_ln(x_ref[...], f, gate_ref, g_ref, b_ref)


def _proj_ln(a_prompt, a_sample, x, mods, layer, w, wslot, g, b):
    t, d = x.shape
    kdim = a_prompt.shape[1]
    tm = 256
    ng = tm // GROUP
    pt = a_prompt.shape[0] // tm
    return pl.pallas_call(
        functools.partial(_proj_ln_kernel, prompt_tiles=pt),
        out_shape=jax.ShapeDtypeStruct((t, d), F32),
        grid=(t // tm,),
        in_specs=[
            pl.BlockSpec((tm, kdim), lambda i: (jnp.minimum(i, pt - 1), 0)),
            pl.BlockSpec((tm, kdim), lambda i: (jnp.maximum(i - pt, 0), 0)),
            pl.BlockSpec((tm, d), lambda i: (i, 0)),
            pl.BlockSpec((1, ng, 1, d), lambda i: (layer, i, 0, 2)),
            pl.BlockSpec((None, kdim, d), lambda i: (wslot, 0, 0)),
            pl.BlockSpec((1, d), lambda i: (0, 0)),
            pl.BlockSpec((1, d), lambda i: (0, 0)),
        ],
        out_specs=pl.BlockSpec((tm, d), lambda i: (i, 0)),
        compiler_params=_cp(40, ("parallel",)),
        name="proj_ln",
    )(a_prompt, a_sample, x, mods, w, g, b)


def _cm_mix_kernel(u_ref, vraw_ref, x_ref, gate_ref, mx_ref, bias_ref, lvg_ref, lvb_ref,
                   wo_ref, g_ref, b_ref, *rest):
    o_ref, vs_ref, vp_ref, gated_ref = rest[-4:]
    tm, width = u_ref.shape
    gw = width // CM_GROUPS
    v = _ln(vraw_ref[...].astype(F32), lvg_ref[...], lvb_ref[...])
    vs_ref[0] = v
    vp_ref[0] = v
    vb = v.astype(BF16)
    for c in range(tm // CM_CHUNK):
        rows = slice(c * CM_CHUNK, (c + 1) * CM_CHUNK)
        for g in range(CM_GROUPS):
            cols = slice(g * gw, (g + 1) * gw)
            mixed = jnp.dot(mx_ref[0, g], vb[rows, cols], preferred_element_type=F32)
            mixed = mixed + bias_ref[0, :, cols]
            gated_ref[rows, cols] = (u_ref[rows, cols].astype(F32) * mixed).astype(BF16)
    f = jnp.dot(gated_ref[...], wo_ref[...], preferred_element_type=F32)
    o_ref[...] = _residual_ln(x_ref[...], f, gate_ref, g_ref, b_ref)


def _cm_mix(zz, x, mods, layer, mx, bias, lvg, lvb, wo, g, b, n_prompt, slot, n_slots, vs_prev, vp_prev):
    t, d = x.shape
    width = zz.shape[1] // 2
    tm = CM_CHUNK
    ng = tm // GROUP
    n_prompt_tiles = n_prompt // tm
    kind = lambda i: jnp.where(i < n_prompt_tiles, 0, 1)
    extra_specs, extra_args, aliases = [], [], {}
    if vs_prev is not None:
        extra_specs = [pl.BlockSpec(memory_space=pl.ANY), pl.BlockSpec(memory_space=pl.ANY)]
        extra_args = [vs_prev, vp_prev]
        aliases = {11: 1, 12: 2}
    return pl.pallas_call(
        _cm_mix_kernel,
        out_shape=(jax.ShapeDtypeStruct((t, d), F32),
                   jax.ShapeDtypeStruct((n_slots, t - n_prompt, width), F32),
                   jax.ShapeDtypeStruct((n_slots, 2 * tm, width), F32)),
        grid=(t // tm,),
        in_specs=[
            pl.BlockSpec((tm, width), lambda i: (i, 0)),
            pl.BlockSpec((tm, width), lambda i: (i, 1)),
            pl.BlockSpec((tm, d), lambda i: (i, 0)),
            pl.BlockSpec((1, ng, 1, d), lambda i: (layer, i, 0, 2)),
            pl.BlockSpec((1, CM_GROUPS, CM_CHUNK, CM_CHUNK), lambda i: (kind(i), 0, 0, 0)),
            pl.BlockSpec((1, CM_CHUNK, width), lambda i: (kind(i), 0, 0)),
            pl.BlockSpec((1, width), lambda i: (0, 0)),
            pl.BlockSpec((1, width), lambda i: (0, 0)),
            pl.BlockSpec((None, width, d), lambda i: (slot, 0, 0), pipeline_mode=pl.Buffered(1)),
            pl.BlockSpec((1, d), lambda i: (0, 0)),
            pl.BlockSpec((1, d), lambda i: (0, 0)),
        ] + extra_specs,
        out_specs=(pl.BlockSpec((tm, d), lambda i: (i, 0)),
                   pl.BlockSpec((1, tm, width), lambda i: (slot, jnp.maximum(i - n_prompt_tiles, 0), 0)),
                   pl.BlockSpec((1, tm, width), lambda i: (slot, kind(i), 0))),
        scratch_shapes=[pltpu.VMEM((tm, width), BF16)],
        input_output_aliases=aliases,
        compiler_params=_cp(48, ("arbitrary",)),
        name="cm_mix",
    )(zz, zz, x, mods, mx, bias, lvg, lvb, wo, g, b, *extra_args)


def _split3(a):
    hi = a.astype(BF16)
    return hi, (a - hi.astype(F32)).astype(BF16)


def _expert_max(a):
    return jnp.max(jnp.max(a, axis=0, keepdims=True), axis=1, keepdims=True)


def _route_kernel(x_ref, sh_ref, sc_ref, wr_ref, br_ref, xm_ref, cw_ref, mk_ref, cnt_ref):
    i = pl.program_id(0)
    xm = _modulate(x_ref[...], sh_ref, sc_ref)
    _store_token_tiles(xm_ref, _pack_halves(xm))
    xh, xl = _split3(xm)
    wh, wl = _split3(wr_ref[...])
    nt = lambda a, b: lax.dot_general(a, b, (((1,), (1,)), ((), ())), preferred_element_type=F32)
    scores = jax.nn.sigmoid(nt(wh, xh) + (nt(wl, xh) + nt(wh, xl)))
    sel = scores + br_ref[...]
    ne, tm = sel.shape
    ng, per = N_EXPERT_GROUPS, ne // N_EXPERT_GROUPS
    sel3 = sel.reshape(ng, per, tm)
    neg = -jnp.inf
    within = lax.broadcasted_iota(I32, (ng, per, tm), 1)
    m1 = jnp.max(sel3, axis=1, keepdims=True)
    i1 = jnp.min(jnp.where(sel3 == m1, within, per), axis=1, keepdims=True)
    m2 = jnp.max(jnp.where(within == i1, neg, sel3), axis=1, keepdims=True)
    gs = m1 + m2
    gidx = lax.broadcasted_iota(I32, (ng, 1, tm), 0)
    rank = jnp.zeros((ng, 1, tm), I32)
    for g in range(ng):
        row = gs[g:g + 1]
        ahead = jnp.logical_or(row > gs, jnp.logical_and(row == gs, g < gidx))
        rank = rank + ahead.astype(I32)
    allowed = (rank < TOPK_GROUPS).astype(F32)
    ms = jnp.where(allowed > 0.0, sel3, neg)
    eidx = lax.broadcasted_iota(I32, (ng, per, tm), 0) * per + within
    chosen = jnp.zeros((ng, per, tm), F32)
    for _ in range(TOP_K):
        best = _expert_max(ms)
        first = -_expert_max(-jnp.where(ms == best, eidx, ne).astype(F32))
        pick = eidx.astype(F32) == first
        chosen = jnp.where(pick, 1.0, chosen)
        ms = jnp.where(pick, neg, ms)
    chosen = chosen.reshape(ne, tm)
    tw = chosen * scores
    cw_ref[...] = tw / jnp.sum(tw, axis=0, keepdims=True) * ROUTED_SCALE
    mk_ref[...] = chosen.astype(BF16)

    @pl.when(i == 0)
    def _():
        cnt_ref[...] = jnp.zeros_like(cnt_ref)

    cnt_ref[...] += jnp.sum(chosen, axis=1, keepdims=True)


def _route(x, mods, layer, wr_t, br_col):
    t, d = x.shape
    ne = wr_t.shape[0]
    tm = 256
    ng = tm // GROUP
    return pl.pallas_call(
        _route_kernel,
        out_shape=(jax.ShapeDtypeStruct((t * TOKEN_ROWS, LANES), U32), jax.ShapeDtypeStruct((ne, t), F32),
                   jax.ShapeDtypeStruct((ne, t), BF16), jax.ShapeDtypeStruct((ne, 1), F32)),
        grid=(t // tm,),
        in_specs=[
            pl.BlockSpec((tm, d), lambda i: (i, 0)),
            pl.BlockSpec((1, ng, 1, d), lambda i: (layer, i, 0, 0)),
            pl.BlockSpec((1, ng, 1, d), lambda i: (layer, i, 0, 1)),
            pl.BlockSpec((ne, d), lambda i: (0, 0)),
            pl.BlockSpec((ne, 1), lambda i: (0, 0)),
        ],
        out_specs=(pl.BlockSpec((tm * TOKEN_ROWS, LANES), lambda i: (i, 0)),
                   pl.BlockSpec((ne, tm), lambda i: (0, i)),
                   pl.BlockSpec((ne, tm), lambda i: (0, i)), pl.BlockSpec((ne, 1), lambda i: (0, 0))),
        compiler_params=_cp(32, ("arbitrary",)),
        name="router",
    )(x, mods, mods, wr_t, br_col)


def _slots_kernel(mk_ref, cw_ref, start_ref, before_ref, lower_ref, pos_ref, w_ref, carry_ref):
    i = pl.program_id(0)

    @pl.when(i == 0)
    def _():
        carry_ref[...] = jnp.zeros_like(carry_ref)

    mk = mk_ref[...]
    rank = jnp.dot(mk, before_ref[...], preferred_element_type=F32)
    pos = start_ref[...] + carry_ref[...] + rank
    order = jnp.dot(lower_ref[...], mk, preferred_element_type=F32)
    chosen = mk > 0
    cw = cw_ref[...]
    rows_p, rows_w = [], []
    for k in range(TOP_K):
        pick = jnp.logical_and(chosen, order == k)
        rows_p.append(jnp.sum(jnp.where(pick, pos, 0.0), axis=0, keepdims=True))
        rows_w.append(jnp.sum(jnp.where(pick, cw, 0.0), axis=0, keepdims=True))
    pos_ref[...] = jnp.concatenate(rows_p, axis=0).astype(I32)
    w_ref[...] = jnp.concatenate(rows_w, axis=0)
    carry_ref[...] += jnp.sum(mk.astype(F32), axis=1, keepdims=True)


def _slots(mk, cw, start_col):
    ne, t = mk.shape
    tm = 256
    r = jnp.arange(tm)
    before = (r[:, None] < r[None, :]).astype(BF16)
    return pl.pallas_call(
        _slots_kernel,
        out_shape=(jax.ShapeDtypeStruct((TOP_K, t), I32), jax.ShapeDtypeStruct((TOP_K, t), F32)),
        grid=(t // tm,),
        in_specs=[
            pl.BlockSpec((ne, tm), lambda i: (0, i)),
            pl.BlockSpec((ne, tm), lambda i: (0, i)),
            pl.BlockSpec((ne, 1), lambda i: (0, 0)),
            pl.BlockSpec((tm, tm), lambda i: (0, 0)),
            pl.BlockSpec((ne, ne), lambda i: (0, 0)),
        ],
        out_specs=(pl.BlockSpec((TOP_K, tm), lambda i: (0, i)), pl.BlockSpec((TOP_K, tm), lambda i: (0, i))),
        scratch_shapes=[pltpu.VMEM((ne, 1), F32)],
        compiler_params=_cp(32, ("arbitrary",)),
        name="plan",
    )(mk, cw, start_col, before, _strict_lower(ne))


def _dispatch_kernel(fill_ref, pad_ref, pos_ref, xm_ref, wsg_ref, wsu_ref, wsd_ref, xs_ref, sh_ref,
                     zeros_ref, sem):
    i = pl.program_id(0)
    tm = xm_ref.shape[0] // TOKEN_ROWS
    ne = fill_ref.shape[0]
    pad_bits = (zeros_ref.shape[0] // TOKEN_ROWS).bit_length() - 1

    def row_copy(r, k):
        return pltpu.make_async_copy(_token_tile(xm_ref, r), _token_tile(xs_ref, pos_ref[r * TOP_K + k]), sem)

    def fill(e, wait):
        first, n = fill_ref[e], pad_ref[e]
        for bit in range(pad_bits):
            size = (1 << bit) * TOKEN_ROWS
            done = (n >> (bit + 1)) << (bit + 1)
            dst = pl.multiple_of((first + done) * TOKEN_ROWS, TOKEN_ROWS)
            piece = pltpu.make_async_copy(zeros_ref.at[pl.ds(0, size)], xs_ref.at[pl.ds(dst, size)], sem)

            @pl.when(((n >> bit) & 1) == 1)
            def _(piece=piece):
                if wait:
                    piece.wait()
                else:
                    piece.start()

    @pl.when(i == 0)
    def _():
        zeros_ref[...] = jnp.zeros_like(zeros_ref)

        def start(e, c):
            fill(e, False)
            return c

        def wait(e, c):
            fill(e, True)
            return c

        lax.fori_loop(0, ne, start, 0)
        lax.fori_loop(0, ne, wait, 0)

    def start(r, c):
        for k in range(TOP_K):
            row_copy(r, k).start(priority=k % 2)
        return c

    def wait(r, c):
        for k in range(TOP_K):
            row_copy(r, k).wait()
        return c

    lax.fori_loop(0, tm, start, 0)
    lo, hi = _unpack_halves(_load_token_tiles(xm_ref))
    xb = jnp.concatenate([lo, hi], axis=1).astype(BF16)
    hg = jnp.dot(xb, wsg_ref[...], preferred_element_type=F32)
    hu = jnp.dot(xb, wsu_ref[...], preferred_element_type=F32)
    f = jnp.dot((jax.nn.silu(hg) * hu).astype(BF16), wsd_ref[...], preferred_element_type=F32)
    sh_ref[...] = f.astype(sh_ref.dtype)
    lax.fori_loop(0, tm, wait, 0)


def _dispatch(fill, pad, pos_flat, xm, wsg, wsu, wsd, layer, n_slots):
    t = xm.shape[0] // TOKEN_ROWS
    d, fs = wsg.shape[1], wsg.shape[2]
    tm = 256
    return pl.pallas_call(
        _dispatch_kernel,
        out_shape=(jax.ShapeDtypeStruct((n_slots * TOKEN_ROWS, LANES), U32),
                   jax.ShapeDtypeStruct((t, d), BF16)),
        grid_spec=pltpu.PrefetchScalarGridSpec(
            num_scalar_prefetch=2,
            grid=(t // tm,),
            in_specs=[
                pl.BlockSpec((tm * TOP_K,), lambda i, fill, pad: (i,), memory_space=pltpu.SMEM),
                pl.BlockSpec((tm * TOKEN_ROWS, LANES), lambda i, fill, pad: (i, 0)),
                pl.BlockSpec((None, d, fs), lambda i, fill, pad: (layer, 0, 0)),
                pl.BlockSpec((None, d, fs), lambda i, fill, pad: (layer, 0, 0)),
                pl.BlockSpec((None, fs, d), lambda i, fill, pad: (layer, 0, 0)),
            ],
            out_specs=(pl.BlockSpec(memory_space=pl.ANY),
                       pl.BlockSpec((tm, d), lambda i, fill, pad: (i, 0))),
            scratch_shapes=[pltpu.VMEM((EXPERT_TILE * TOKEN_ROWS, LANES), U32), pltpu.SemaphoreType.DMA(())],
        ),
        compiler_params=_cp(40, ("arbitrary",)),
        name="dispatch",
    )(fill, pad, pos_flat, xm, wsg, wsu, wsd)


def _gmm_kernel(te_ref, nt_ref, nxt_ref, slot_ref, xs_ref, wg_ref, wu_ref, wd_ref, ys_ref,
                fg_ref, fu_ref, fd_ref, wgb_ref, wub_ref, wdb_ref, sems, *, layer):
    i = pl.program_id(0)
    e = te_ref[i]
    s = slot_ref[i]
    prev = te_ref[jnp.maximum(i - 1, 0)]

    def fetch(expert, slot):
        return (pltpu.make_async_copy(wg_ref.at[layer, expert], fg_ref.at[slot], sems.at[slot]),
                pltpu.make_async_copy(wu_ref.at[layer, expert], fu_ref.at[slot], sems.at[slot]),
                pltpu.make_async_copy(wd_ref.at[layer, expert], fd_ref.at[slot], sems.at[slot]))

    @pl.when(i == 0)
    def _():
        for c in fetch(e, s):
            c.start()

    for slot in range(2):
        @pl.when(jnp.logical_and(jnp.logical_or(i == 0, e != prev), s == slot))
        def _(slot=slot):
            for c in fetch(e, slot):
                c.wait()
            wgb_ref[...] = fg_ref[slot].astype(BF16)
            wub_ref[...] = fu_ref[slot].astype(BF16)
            wdb_ref[...] = fd_ref[slot].astype(BF16)

            @pl.when(nxt_ref[i] >= 0)
            def _():
                for c in fetch(nxt_ref[i], 1 - slot):
                    c.start()

    @pl.when(i < nt_ref[0])
    def _():
        lo, hi = _unpack_halves(_load_token_tiles(xs_ref))
        x = jnp.concatenate([lo, hi], axis=1).astype(BF16)
        hg = jnp.dot(x, wgb_ref[...], preferred_element_type=F32)
        hu = jnp.dot(x, wub_ref[...], preferred_element_type=F32)
        hid = (jax.nn.silu(hg) * hu).astype(BF16)
        y = jnp.dot(hid, wdb_ref[...], preferred_element_type=F32)
        _store_token_tiles(ys_ref, _pack_halves(y))


def _gmm(tile_expert, n_tiles_used, tile_next, tile_slot, xs, wg, wu, wd, layer, max_tiles):
    d, f = wg.shape[2], wg.shape[3]
    tm = EXPERT_TILE
    row = lambda i, te, nt, nx, sl: (jnp.minimum(i, nt[0] - 1), 0)
    return pl.pallas_call(
        functools.partial(_gmm_kernel, layer=layer),
        out_shape=jax.ShapeDtypeStruct(xs.shape, U32),
        grid_spec=pltpu.PrefetchScalarGridSpec(
            num_scalar_prefetch=4,
            grid=(max_tiles,),
            in_specs=[
                pl.BlockSpec((tm * TOKEN_ROWS, LANES), row),
                pl.BlockSpec(memory_space=pl.ANY),
                pl.BlockSpec(memory_space=pl.ANY),
                pl.BlockSpec(memory_space=pl.ANY),
            ],
            out_specs=pl.BlockSpec((tm * TOKEN_ROWS, LANES), row),
            scratch_shapes=[pltpu.VMEM((2, d, f), F32), pltpu.VMEM((2, d, f), F32), pltpu.VMEM((2, f, d), F32),
                            pltpu.VMEM((d, f), BF16), pltpu.VMEM((d, f), BF16), pltpu.VMEM((f, d), BF16),
                            pltpu.SemaphoreType.DMA((2,))],
        ),
        compiler_params=_cp(56, ("arbitrary",)),
        name="gmm",
    )(tile_expert, n_tiles_used, tile_next, tile_slot, xs, wg, wu, wd)


def _combine_kernel(pos_ref, nxt_ref, w_ref, sh_ref, x_ref, gate_ref, g_ref, b_ref, ys_ref, o_ref,
                    buf_ref, sems):
    i = pl.program_id(0)
    n = pl.num_programs(0)
    tm, d = x_ref.shape
    half = d // 2
    slot = i % 2

    def copy(p_ref, s, r, k):
        return pltpu.make_async_copy(_token_tile(ys_ref, p_ref[r * TOP_K + k]),
                                     _token_tile(buf_ref.at[s, k], r), sems.at[s])

    def gather(p_ref, s):
        def start(r, c):
            for k in range(TOP_K):
                copy(p_ref, s, r, k).start(priority=k % 2)
            return c
        lax.fori_loop(0, tm, start, 0)

    @pl.when(i == 0)
    def _():
        gather(pos_ref, 0)

    for s in range(2):
        @pl.when(jnp.logical_and(i + 1 < n, slot == 1 - s))
        def _(s=s):
            gather(nxt_ref, s)

    def wait(r, c):
        for k in range(TOP_K):
            copy(pos_ref, slot, r, k).wait()
        return c

    lax.fori_loop(0, tm, wait, 0)
    f = sh_ref[...].astype(F32)
    w = w_ref[...]
    wk = [jnp.broadcast_to(w[:, k:k + 1], (tm, LANES)) for k in range(TOP_K)]
    los, his = [], []
    for c in range(TOKEN_ROWS):
        acc_lo = f[:, c * LANES:(c + 1) * LANES]
        acc_hi = f[:, half + c * LANES:half + (c + 1) * LANES]
        for k in range(TOP_K):
            lo, hi = _unpack_halves(buf_ref[slot, k, pl.ds(c, tm, stride=TOKEN_ROWS), :])
            acc_lo = acc_lo + wk[k] * lo
            acc_hi = acc_hi + wk[k] * hi
        los.append(acc_lo)
        his.append(acc_hi)
    f = jnp.concatenate(los + his, axis=1)
    o_ref[...] = _residual_ln(x_ref[...], f, gate_ref, g_ref, b_ref)


def _combine(pos_flat, w, shared, x, mods, layer, g, b, ys, row0, n_rows):
    d = x.shape[1]
    tm = 128
    ng = tm // GROUP
    off = row0 // tm
    last = n_rows // tm - 1
    return pl.pallas_call(
        _combine_kernel,
        out_shape=jax.ShapeDtypeStruct((n_rows, d), F32),
        grid=(n_rows // tm,),
        in_specs=[
            pl.BlockSpec((tm * TOP_K,), lambda i: (i + off,), memory_space=pltpu.SMEM),
            pl.BlockSpec((tm * TOP_K,), lambda i: (jnp.minimum(i + 1, last) + off,), memory_space=pltpu.SMEM),
            pl.BlockSpec((tm, TOP_K), lambda i: (i + off, 0)),
            pl.BlockSpec((tm, d), lambda i: (i + off, 0)),
            pl.BlockSpec((tm, d), lambda i: (i + off, 0)),
            pl.BlockSpec((1, ng, 1, d), lambda i: (layer, i + off, 0, 2)),
            pl.BlockSpec((1, d), lambda i: (0, 0)),
            pl.BlockSpec((1, d), lambda i: (0, 0)),
            pl.BlockSpec(memory_space=pl.ANY),
        ],
        out_specs=pl.BlockSpec((tm, d), lambda i: (i, 0)),
        scratch_shapes=[pltpu.VMEM((2, TOP_K, tm * TOKEN_ROWS, LANES), U32), pltpu.SemaphoreType.DMA((2,))],
        compiler_params=_cp(48, ("arbitrary",)),
        name="combine",
    )(pos_flat, pos_flat, w, shared, x, mods, g, b, ys)


def _moe(x, mods, layer, wr, br, wg, wu, wd, wsg, wsu, wsd, g, b, splits):
    t = x.shape[0]
    ne = wr.shape[1]
    tile = EXPERT_TILE
    max_tiles = (t * TOP_K) // tile + ne
    xm, cw, mk, cnt = _route(x, mods, layer, wr.T, br.reshape(ne, 1))
    ids = jnp.arange(ne, dtype=I32)
    upto = ids[None, :] <= ids[:, None]
    cnt = cnt[:, 0].astype(I32)
    tiles = (cnt + tile - 1) // tile
    ends = jnp.sum(jnp.where(upto, tiles[None, :], 0), axis=1)
    start = (ends - tiles) * tile
    n_used = ends[-1:]
    tile_ids = jnp.minimum(jnp.arange(max_tiles, dtype=I32), n_used[0] - 1)
    tile_expert = jnp.sum((ends[None, :] <= tile_ids[:, None]).astype(I32), axis=1)
    used = tiles > 0
    later_used = jnp.logical_and(ids[None, :] > ids[:, None], used[None, :])
    next_used = jnp.min(jnp.where(later_used, ids[None, :], ne), axis=1)
    next_used = jnp.where(next_used < ne, next_used, -1).astype(I32)
    slot = ((jnp.sum(jnp.logical_and(upto, used[None, :]).astype(I32), axis=1) - 1) % 2).astype(I32)
    pos, w = _slots(mk, cw, start.astype(F32)[:, None])
    pos_flat = pos.T.reshape(-1)
    w = w.T
    xs, shared = _dispatch(start + cnt, tiles * tile - cnt, pos_flat, xm, wsg, wsu, wsd, layer,
                           max_tiles * tile)
    of_tile = (tile_expert[:, None] == ids[None, :]).astype(I32)
    tile_next = jnp.sum(of_tile * next_used[None, :], axis=1)
    tile_slot = jnp.sum(of_tile * slot[None, :], axis=1)
    ys = _gmm(tile_expert, n_used.astype(I32), tile_next, tile_slot, xs, wg, wu, wd, layer, max_tiles)
    return [_combine(pos_flat, w, shared, x, mods, layer, g, b, ys, row0, n_rows)
            for row0, n_rows in splits]


def _expand_mods(mod, n_prompt_groups, n_streams):
    depth, _, n = mod.shape
    p = jnp.broadcast_to(mod[:, 0:1], (depth, n_prompt_groups, n))
    return jnp.concatenate([p, mod[:, 1:1 + n_streams]], axis=1)[:, :, None, :]


def kernel(x_prompt, x_sample, c_prompt, c_sample, cache_k, cache_v, w_ada_mix, b_ada_mix, w_ada_ffn, b_ada_ffn, ln_mix_g, ln_mix_b, ln_ffn_g, ln_ffn_b, w_qkv, w_o, w_uv, b_uv, ln_v_g, ln_v_b, w_s, b_s, w_cm_out, w_router, b_router, w_gate, w_up, w_down, ws_gate, ws_up, ws_down):
    bp, sp, d = x_prompt.shape
    bs, t_new, _ = x_sample.shape
    assert bp == 1 and t_new == GROUP and sp % ATT_BLOCK == 0 and d == N_HEADS * HEAD_DIM
    n_prompt = bp * sp
    n_sample = bs * t_new
    width = w_uv.shape[2] // 2
    past = cache_k.shape[2]

    x = jnp.concatenate([x_prompt.reshape(n_prompt, d), x_sample.reshape(n_sample, d)], axis=0)

    rows = 8 * ((1 + bs + 7) // 8)
    c_all = jnp.zeros((rows, d), F32).at[0:1].set(c_prompt).at[1:1 + bs].set(c_sample)
    mods_mix = _expand_mods(_ada(c_all, w_ada_mix, b_ada_mix), n_prompt // GROUP, bs)
    mods_ffn = _expand_mods(_ada(c_all, w_ada_ffn, b_ada_ffn), n_prompt // GROUP, bs)

    cpos = jnp.arange(CM_CHUNK)
    cmask = (cpos[None, :] // CHUNK) <= (cpos[:, None] // CHUNK)
    pair = (cpos[None, :] // t_new) == (cpos[:, None] // t_new)
    fold = cpos % t_new
    smask = (fold[None, :] // CHUNK) <= (fold[:, None] // CHUNK)
    ws_prompt = jnp.where(cmask, w_s, 0.0)
    ws_sample = jnp.where(jnp.logical_and(pair, smask), w_s[:, :, fold][:, :, :, fold], 0.0)
    mix = jnp.stack([ws_prompt, ws_sample], axis=1).astype(BF16)
    gw = width // CM_GROUPS
    bias_p = jnp.repeat(jnp.swapaxes(b_s, 1, 2), gw, axis=2)
    mix_bias = jnp.stack([bias_p, bias_p[:, fold]], axis=1)

    ck = cache_k.reshape(cache_k.shape[0], bs, past * N_HEADS, HEAD_DIM)
    cv = cache_v.reshape(cache_v.shape[0], bs, past * N_HEADS, HEAD_DIM)

    n_sb = (DEPTH + 1) // 2
    n_cm = DEPTH // 2
    kp = vp = ksm = vsm = cms = cmp_ = None
    wq, wo = w_qkv.astype(BF16), w_o.astype(BF16)
    wuv, wcm = w_uv.astype(BF16), w_cm_out.astype(BF16)
    wsg, wsu, wsd = ws_gate.astype(BF16), ws_up.astype(BF16), ws_down.astype(BF16)
    buv = b_uv[:, None, :]
    for i in range(DEPTH):
        j = i // 2
        if i % 2 == 0:
            qkv_p, kp, vp = _qkv(x, mods_mix, i, wq, 0, n_prompt, j, n_sb, kp, vp)
            qkv_s, ksm, vsm = _qkv(x, mods_mix, i, wq, n_prompt, n_sample, j, n_sb, ksm, vsm)
            o_p = _attn_prompt(qkv_p, n_prompt)
            o_s, cs = _attn_sample(qkv_s, ck, cv, j, 0, bs, t_new)
            o_s = lax.cond(
                jnp.max(cs) > STICK_DONE,
                lambda: _attn_sample_rest(qkv_s, o_s, cs, ck, cv, j, 0, bs, t_new),
                lambda: o_s)
            x = _proj_ln(o_p, o_s, x, mods_mix, i, wo, j, ln_mix_g[i][None], ln_mix_b[i][None])
        else:
            zz = _modmm(x, mods_mix, i, wuv, buv, j, BF16, True)
            x, cms, cmp_ = _cm_mix(zz, x, mods_mix, i, mix[j], mix_bias[j], ln_v_g[j][None], ln_v_b[j][None],
                                   wcm, ln_mix_g[i][None], ln_mix_b[i][None],
                                   n_prompt, j, n_cm, cms, cmp_)
        last = i == DEPTH - 1
        splits = [(0, n_prompt), (n_prompt, n_sample)] if last else [(0, n_prompt + n_sample)]
        outs = _moe(x, mods_ffn, i, w_router[i], b_router[i][None], w_gate, w_up, w_down,
                    wsg, wsu, wsd, ln_ffn_g[i][None], ln_ffn_b[i][None], splits)
        x = outs[0]

    y_prompt, y_sample = outs
    return (
        y_prompt.reshape(bp, sp, d),
        y_sample.reshape(bs, t_new, d),
        kp.reshape(n_sb, bp, sp, N_HEADS, HEAD_DIM),
        vp.reshape(n_sb, bp, sp, N_HEADS, HEAD_DIM),
        ksm.reshape(n_sb, bs, t_new, N_HEADS, HEAD_DIM),
        vsm.reshape(n_sb, bs, t_new, N_HEADS, HEAD_DIM),
        cmp_[:, :CM_CHUNK].reshape(n_cm, bp, CM_CHUNK, width),
        cms.reshape(n_cm, bs, t_new, width),
    )
```

```python
import functools

import jax
import jax.numpy as jnp
from jax import lax
from jax.experimental import pallas as pl
from jax.experimental.pallas import tpu as pltpu

F32 = jnp.float32
BF16 = jnp.bfloat16
I32 = jnp.int32

DEPTH = 4
N_HEADS = 16
HEAD_DIM = 128
CHUNK = 64
CM_CHUNK = 128
CM_GROUPS = 16
N_EXPERTS = 64
N_EXPERT_GROUPS = 8
TOPK_GROUPS = 4
TOP_K = 8
ROUTED_SCALE = 2.5
ALPHA = (2 * DEPTH) ** 0.25
LN_EPS = 1e-5

GROUP = 64
MIB = 1024 * 1024
STICK_DONE = -104.0
ATT_BLOCK = 256
EXPERT_TILE = 256


def _cp(vmem_mib, sem):
    return pltpu.CompilerParams(dimension_semantics=sem, vmem_limit_bytes=vmem_mib * MIB)


def _ln(y, g, b):
    mu = jnp.mean(y, axis=-1, keepdims=True)
    yc = y - mu
    var = jnp.mean(yc * yc, axis=-1, keepdims=True)
    return yc * lax.rsqrt(var + LN_EPS) * g + b


def _modulate(x, sh_ref, sc_ref):
    tm, d = x.shape
    g = sh_ref.shape[1]
    xg = x.reshape(g, tm // g, d)
    return (xg * sc_ref[0] + sh_ref[0]).reshape(tm, d)


def _residual_ln(x, f, gate_ref, g_ref, b_ref):
    tm, d = x.shape
    g = gate_ref.shape[1]
    y = ALPHA * x + (gate_ref[0] * f.reshape(g, tm // g, d)).reshape(tm, d)
    return _ln(y, g_ref[...], b_ref[...])


TOKEN_ROWS = 8
LANES = 128
U32 = jnp.uint32


def _pack_halves(y):
    n = y.shape[1] // 2
    lo = lax.bitcast_convert_type(y[:, :n].astype(BF16).astype(F32), U32)
    hi = lax.bitcast_convert_type(y[:, n:].astype(BF16).astype(F32), U32)
    return hi | (lo >> 16)


def _unpack_halves(w):
    lo = lax.bitcast_convert_type(w << 16, F32)
    hi = lax.bitcast_convert_type(w & jnp.uint32(0xFFFF0000), F32)
    return lo, hi


def _store_token_tiles(ref, words):
    tm = words.shape[0]
    for c in range(TOKEN_ROWS):
        ref[pl.ds(c, tm, stride=TOKEN_ROWS), :] = words[:, c * LANES:(c + 1) * LANES]


def _load_token_tiles(ref):
    tm = ref.shape[0] // TOKEN_ROWS
    return jnp.concatenate([ref[pl.ds(c, tm, stride=TOKEN_ROWS), :] for c in range(TOKEN_ROWS)], axis=1)


def _token_tile(ref, i):
    return ref.at[pl.ds(pl.multiple_of(i * TOKEN_ROWS, TOKEN_ROWS), TOKEN_ROWS)]


def _ada_kernel(c_ref, w_ref, b_ref, o_ref, *, d_model, tn):
    j = pl.program_id(1)
    a = jax.nn.silu(c_ref[...]).astype(BF16)
    acc = jnp.dot(a, w_ref[0].astype(BF16), preferred_element_type=F32) + b_ref[0]
    o_ref[0] = acc + jnp.where(j * tn >= d_model, 1.0, 0.0).astype(F32)


def _ada(c_all, w, b):
    depth, d, n = w.shape
    r = c_all.shape[0]
    tn = 1024
    return pl.pallas_call(
        functools.partial(_ada_kernel, d_model=d, tn=tn),
        out_shape=jax.ShapeDtypeStruct((depth, r, n), F32),
        grid=(depth, n // tn),
        in_specs=[
            pl.BlockSpec((r, d), lambda l, j: (0, 0)),
            pl.BlockSpec((1, d, tn), lambda l, j: (l, 0, j)),
            pl.BlockSpec((1, 1, tn), lambda l, j: (l, 0, j)),
        ],
        out_specs=pl.BlockSpec((1, r, tn), lambda l, j: (l, 0, j)),
        compiler_params=_cp(40, ("parallel", "parallel")),
        name="ada",
    )(c_all, w, b.reshape(depth, 1, n))


def _modmm_kernel(x_ref, sh_ref, sc_ref, w_ref, b_ref, o_ref, *, gelu):
    h = _modulate(x_ref[...], sh_ref, sc_ref).astype(BF16)
    acc = jnp.dot(h, w_ref[...], preferred_element_type=F32) + b_ref[...]
    if gelu:
        acc = jax.nn.gelu(acc)
    o_ref[...] = acc.astype(o_ref.dtype)


def _modmm(x, mods, layer, w, b, wslot, out_dtype, gelu):
    t, d = x.shape
    n = w.shape[2]
    tm, tn = 512, 2048
    g = tm // GROUP
    return pl.pallas_call(
        functools.partial(_modmm_kernel, gelu=gelu),
        out_shape=jax.ShapeDtypeStruct((t, n), out_dtype),
        grid=(n // tn, t // tm),
        in_specs=[
            pl.BlockSpec((tm, d), lambda j, i: (i, 0)),
            pl.BlockSpec((1, g, 1, d), lambda j, i: (layer, i, 0, 0)),
            pl.BlockSpec((1, g, 1, d), lambda j, i: (layer, i, 0, 1)),
            pl.BlockSpec((None, d, tn), lambda j, i: (wslot, 0, j)),
            pl.BlockSpec((None, 1, tn), lambda j, i: (wslot, 0, j)),
        ],
        out_specs=pl.BlockSpec((tm, tn), lambda j, i: (i, j)),
        compiler_params=_cp(48, ("parallel", "parallel")),
        name="modmm",
    )(x, mods, mods, w, b)


def _qkv_kernel(x_ref, sh_ref, sc_ref, w_ref, *rest):
    qkv_ref, kf_ref, vf_ref = rest[-3:]
    h = _modulate(x_ref[...], sh_ref, sc_ref).astype(BF16)
    acc = jnp.dot(h, w_ref[...], preferred_element_type=F32)
    qkv_ref[...] = acc.astype(BF16)
    tm = acc.shape[0]
    d = acc.shape[1] // 3
    for hd in range(N_HEADS):
        cols = slice(hd * HEAD_DIM, (hd + 1) * HEAD_DIM)
        kf_ref[0, pl.ds(hd, tm, stride=N_HEADS), :] = acc[:, d:2 * d][:, cols]
        vf_ref[0, pl.ds(hd, tm, stride=N_HEADS), :] = acc[:, 2 * d:][:, cols]


def _qkv(x, mods, layer, w, row0, n_rows, slot, n_slots, kf_prev, vf_prev):
    d = x.shape[1]
    tm = 256
    g = tm // GROUP
    off = row0 // tm
    flat = (n_slots, n_rows * N_HEADS, HEAD_DIM)
    in_specs = [
        pl.BlockSpec((tm, d), lambda i: (i + off, 0)),
        pl.BlockSpec((1, g, 1, d), lambda i: (layer, i + off, 0, 0)),
        pl.BlockSpec((1, g, 1, d), lambda i: (layer, i + off, 0, 1)),
        pl.BlockSpec((None, d, 3 * d), lambda i: (slot, 0, 0), pipeline_mode=pl.Buffered(1)),
    ]
    args = [x, mods, mods, w]
    aliases = {}
    if kf_prev is not None:
        in_specs += [pl.BlockSpec(memory_space=pl.ANY), pl.BlockSpec(memory_space=pl.ANY)]
        args += [kf_prev, vf_prev]
        aliases = {4: 1, 5: 2}
    return pl.pallas_call(
        _qkv_kernel,
        out_shape=(jax.ShapeDtypeStruct((n_rows, 3 * d), BF16),
                   jax.ShapeDtypeStruct(flat, F32), jax.ShapeDtypeStruct(flat, F32)),
        grid=(n_rows // tm,),
        in_specs=in_specs,
        out_specs=(pl.BlockSpec((tm, 3 * d), lambda i: (i, 0)),
                   pl.BlockSpec((1, tm * N_HEADS, HEAD_DIM), lambda i: (slot, i, 0)),
                   pl.BlockSpec((1, tm * N_HEADS, HEAD_DIM), lambda i: (slot, i, 0))),
        input_output_aliases=aliases,
        compiler_params=_cp(56, ("parallel",)),
        name="qkv",
    )(*args)


def _sb_block(q, k, v, csum, u, mask):
    z = lax.dot_general(q, k, (((1,), (1,)), ((), ())), preferred_element_type=F32)
    z = z * (HEAD_DIM ** -0.5)
    t = jnp.log(1.0 + jnp.exp(-jnp.abs(z)))
    log_beta = -(jnp.maximum(-z, 0.0) + t)
    l1 = log_beta - z
    if mask is not None:
        l1 = jnp.where(mask, l1, 0.0)
    hi = l1.astype(BF16)
    lo = (l1 - hi.astype(F32)).astype(BF16)
    s = jnp.dot(hi, u, preferred_element_type=F32) + jnp.dot(lo, u, preferred_element_type=F32)
    a = jnp.exp(log_beta + s + csum)
    if mask is not None:
        a = jnp.where(mask, a, 0.0)
    o = jnp.dot(a.astype(BF16), v, preferred_element_type=F32)
    return o, csum + s[:, :1] + l1[:, :1]


def _causal_mask(n):
    row = lax.broadcasted_iota(I32, (n, n), 0)
    col = lax.broadcasted_iota(I32, (n, n), 1)
    return col < row


def _attn_prompt_kernel(q_ref, k_ref, v_ref, u_ref, o_ref):
    i = pl.program_id(1)
    bq = q_ref.shape[0]
    heads = q_ref.shape[1] // HEAD_DIM
    u = u_ref[...]
    mask = _causal_mask(bq)

    def kv(b, cols):
        start = pl.multiple_of(b * bq, bq)
        return (k_ref[pl.ds(start, bq), cols].astype(BF16), v_ref[pl.ds(start, bq), cols].astype(BF16))

    prev = jnp.maximum(i - 1, 0)
    has_prev = i > 0
    qs, os, css = [], [], []
    for h in range(heads):
        cols = slice(h * HEAD_DIM, (h + 1) * HEAD_DIM)
        q = q_ref[:, cols].astype(BF16)
        k0, v0 = kv(i, cols)
        o, cs = _sb_block(q, k0, v0, jnp.zeros((bq, 1), F32), u, mask)
        k1, v1 = kv(prev, cols)
        do, cs1 = _sb_block(q, k1, v1, cs, u, None)
        qs.append(q)
        os.append(o + jnp.where(has_prev, do, 0.0))
        css.append(jnp.where(has_prev, cs1, cs))

    def cond(c):
        b, _, css = c
        live = jnp.max(css[0])
        for cs in css[1:]:
            live = jnp.maximum(live, jnp.max(cs))
        return jnp.logical_and(b >= 0, live > STICK_DONE)

    def body(c):
        b, os, css = c
        new_os, new_css = [], []
        for h in range(heads):
            cols = slice(h * HEAD_DIM, (h + 1) * HEAD_DIM)
            kb, vb = kv(b, cols)
            do, cs = _sb_block(qs[h], kb, vb, css[h], u, None)
            new_os.append(os[h] + do)
            new_css.append(cs)
        return b - 1, tuple(new_os), tuple(new_css)

    _, os, _ = lax.while_loop(cond, body, (i - 2, tuple(os), tuple(css)))
    for h in range(heads):
        o_ref[:, h * HEAD_DIM:(h + 1) * HEAD_DIM] = os[h].astype(o_ref.dtype)


def _strict_lower(n):
    r = jnp.arange(n)
    return (r[:, None] > r[None, :]).astype(BF16)


def _attn_prompt(qkv, n_prompt):
    d = N_HEADS * HEAD_DIM
    bq = ATT_BLOCK
    hp = 4
    groups = N_HEADS // hp
    wcols = hp * HEAD_DIM
    return pl.pallas_call(
        _attn_prompt_kernel,
        out_shape=jax.ShapeDtypeStruct((n_prompt, d), BF16),
        grid=(groups, n_prompt // bq),
        in_specs=[
            pl.BlockSpec((bq, wcols), lambda h, i: (i, h)),
            pl.BlockSpec((n_prompt, wcols), lambda h, i: (0, groups + h)),
            pl.BlockSpec((n_prompt, wcols), lambda h, i: (0, 2 * groups + h)),
            pl.BlockSpec((bq, bq), lambda h, i: (0, 0)),
        ],
        out_specs=pl.BlockSpec((bq, wcols), lambda h, i: (i, h)),
        compiler_params=_cp(48, ("parallel", "parallel")),
        name="attn_prompt",
    )(qkv, qkv, qkv, _strict_lower(bq))


def _attn_sample_kernel(q_ref, kn_ref, vn_ref, ck_ref, cv_ref, un_ref, uc_ref, o_ref, cs_ref):
    tq = q_ref.shape[0]
    bk = uc_ref.shape[0]
    mask = _causal_mask(tq)
    for h in range(N_HEADS):
        cols = slice(h * HEAD_DIM, (h + 1) * HEAD_DIM)
        q = q_ref[:, cols].astype(BF16)
        o, cs = _sb_block(q, kn_ref[:, cols].astype(BF16), vn_ref[:, cols].astype(BF16),
                          jnp.zeros((tq, 1), F32), un_ref[...], mask)
        kc = ck_ref[0, 0, pl.ds(h, bk, stride=N_HEADS), :].astype(BF16)
        vc = cv_ref[0, 0, pl.ds(h, bk, stride=N_HEADS), :].astype(BF16)
        do, cs = _sb_block(q, kc, vc, cs, uc_ref[...], None)
        o_ref[:, cols] = o + do
        cs_ref[:, cols] = jnp.broadcast_to(cs, (tq, HEAD_DIM))


def _attn_sample(qkv, cache_k, cache_v, layer, n_prompt, n_streams, t_new):
    d = N_HEADS * HEAD_DIM
    past = cache_k.shape[2] // N_HEADS
    bk = ATT_BLOCK
    pb = n_prompt // t_new
    last = past // bk - 1
    return pl.pallas_call(
        _attn_sample_kernel,
        out_shape=(jax.ShapeDtypeStruct((n_streams * t_new, d), F32),
                   jax.ShapeDtypeStruct((n_streams * t_new, d), F32)),
        grid=(n_streams,),
        in_specs=[
            pl.BlockSpec((t_new, d), lambda s: (pb + s, 0)),
            pl.BlockSpec((t_new, d), lambda s: (pb + s, 1)),
            pl.BlockSpec((t_new, d), lambda s: (pb + s, 2)),
            pl.BlockSpec((1, 1, bk * N_HEADS, HEAD_DIM), lambda s: (layer, s, last, 0)),
            pl.BlockSpec((1, 1, bk * N_HEADS, HEAD_DIM), lambda s: (layer, s, last, 0)),
            pl.BlockSpec((t_new, t_new), lambda s: (0, 0)),
            pl.BlockSpec((bk, bk), lambda s: (0, 0)),
        ],
        out_specs=(pl.BlockSpec((t_new, d), lambda s: (s, 0)),
                   pl.BlockSpec((t_new, d), lambda s: (s, 0))),
        compiler_params=_cp(32, ("parallel",)),
        name="attn_sample",
    )(qkv, qkv, qkv, cache_k, cache_v, _strict_lower(t_new), _strict_lower(bk))


def _attn_sample_rest_kernel(q_ref, oin_ref, csin_ref, ck_ref, cv_ref, u_ref, o_ref, cs_ref):
    b = pl.program_id(1)
    bk = u_ref.shape[0]

    @pl.when(b == 0)
    def _():
        o_ref[...] = oin_ref[...]
        cs_ref[...] = csin_ref[...]

    for h in range(N_HEADS):
        cols = slice(h * HEAD_DIM, (h + 1) * HEAD_DIM)
        cs = cs_ref[:, cols][:, :1]

        @pl.when(jnp.max(cs) > STICK_DONE)
        def _(h=h, cols=cols, cs=cs):
            q = q_ref[:, cols].astype(BF16)
            kc = ck_ref[0, 0, pl.ds(h, bk, stride=N_HEADS), :].astype(BF16)
            vc = cv_ref[0, 0, pl.ds(h, bk, stride=N_HEADS), :].astype(BF16)
            do, cs2 = _sb_block(q, kc, vc, cs, u_ref[...], None)
            o_ref[:, cols] += do
            cs_ref[:, cols] = jnp.broadcast_to(cs2, (cs2.shape[0], HEAD_DIM))


def _attn_sample_rest(qkv, o_part, cs_part, cache_k, cache_v, layer, n_prompt, n_streams, t_new):
    d = N_HEADS * HEAD_DIM
    past = cache_k.shape[2] // N_HEADS
    bk = ATT_BLOCK
    pb = n_prompt // t_new
    nb = past // bk - 1
    o, _ = pl.pallas_call(
        _attn_sample_rest_kernel,
        out_shape=(jax.ShapeDtypeStruct((n_streams * t_new, d), F32),
                   jax.ShapeDtypeStruct((n_streams * t_new, d), F32)),
        grid=(n_streams, nb),
        in_specs=[
            pl.BlockSpec((t_new, d), lambda s, b: (pb + s, 0)),
            pl.BlockSpec((t_new, d), lambda s, b: (s, 0)),
            pl.BlockSpec((t_new, d), lambda s, b: (s, 0)),
            pl.BlockSpec((1, 1, bk * N_HEADS, HEAD_DIM), lambda s, b: (layer, s, nb - 1 - b, 0)),
            pl.BlockSpec((1, 1, bk * N_HEADS, HEAD_DIM), lambda s, b: (layer, s, nb - 1 - b, 0)),
            pl.BlockSpec((bk, bk), lambda s, b: (0, 0)),
        ],
        out_specs=(pl.BlockSpec((t_new, d), lambda s, b: (s, 0)),
                   pl.BlockSpec((t_new, d), lambda s, b: (s, 0))),
        compiler_params=_cp(32, ("parallel", "arbitrary")),
        name="attn_sample_rest",
    )(qkv, o_part, cs_part, cache_k, cache_v, _strict_lower(bk))
    return o


def _proj_ln_kernel(ap_ref, as_ref, x_ref, gate_ref, w_ref, g_ref, b_ref, o_ref, *, prompt_tiles):
    a = jnp.where(pl.program_id(0) < prompt_tiles, ap_ref[...], as_ref[...].astype(BF16))
    f = jnp.dot(a, w_ref[...], preferred_element_type=F32)
    o_ref[...] = _residual_ln(x_ref[...], f, gate_ref, g_ref, b_ref)


def _proj_ln(a_prompt, a_sample, x, mods, layer, w, wslot, g, b):
    t, d = x.shape
    kdim = a_prompt.shape[1]
    tm = 256
    ng = tm // GROUP
    pt = a_prompt.shape[0] // tm
    return pl.pallas_call(
        functools.partial(_proj_ln_kernel, prompt_tiles=pt),
        out_shape=jax.ShapeDtypeStruct((t, d), F32),
        grid=(t // tm,),
        in_specs=[
            pl.BlockSpec((tm, kdim), lambda i: (jnp.minimum(i, pt - 1), 0)),
            pl.BlockSpec((tm, kdim), lambda i: (jnp.maximum(i - pt, 0), 0)),
            pl.BlockSpec((tm, d), lambda i: (i, 0)),
            pl.BlockSpec((1, ng, 1, d), lambda i: (layer, i, 0, 2)),
            pl.BlockSpec((None, kdim, d), lambda i: (wslot, 0, 0)),
            pl.BlockSpec((1, d), lambda i: (0, 0)),
            pl.BlockSpec((1, d), lambda i: (0, 0)),
        ],
        out_specs=pl.BlockSpec((tm, d), lambda i: (i, 0)),
        compiler_params=_cp(40, ("parallel",)),
        name="proj_ln",
    )(a_prompt, a_sample, x, mods, w, g, b)


def _cm_mix_kernel(u_ref, vraw_ref, x_ref, gate_ref, mx_ref, bias_ref, lvg_ref, lvb_ref,
                   wo_ref, g_ref, b_ref, *rest):
    o_ref, vs_ref, vp_ref, gated_ref = rest[-4:]
    tm, width = u_ref.shape
    gw = width // CM_GROUPS
    v = _ln(vraw_ref[...].astype(F32), lvg_ref[...], lvb_ref[...])
    vs_ref[0] = v
    vp_ref[0] = v
    vb = v.astype(BF16)
    for c in range(tm // CM_CHUNK):
        rows = slice(c * CM_CHUNK, (c + 1) * CM_CHUNK)
        for g in range(CM_GROUPS):
            cols = slice(g * gw, (g + 1) * gw)
            mixed = jnp.dot(mx_ref[0, g], vb[rows, cols], preferred_element_type=F32)
            mixed = mixed + bias_ref[0, :, cols]
            gated_ref[rows, cols] = (u_ref[rows, cols].astype(F32) * mixed).astype(BF16)
    f = jnp.dot(gated_ref[...], wo_ref[...], preferred_element_type=F32)
    o_ref[...] = _residual_ln(x_ref[...], f, gate_ref, g_ref, b_ref)


def _cm_mix(zz, x, mods, layer, mx, bias, lvg, lvb, wo, g, b, n_prompt, slot, n_slots, vs_prev, vp_prev):
    t, d = x.shape
    width = zz.shape[1] // 2
    tm = CM_CHUNK
    ng = tm // GROUP
    n_prompt_tiles = n_prompt // tm
    kind = lambda i: jnp.where(i < n_prompt_tiles, 0, 1)
    extra_specs, extra_args, aliases = [], [], {}
    if vs_prev is not None:
        extra_specs = [pl.BlockSpec(memory_space=pl.ANY), pl.BlockSpec(memory_space=pl.ANY)]
        extra_args = [vs_prev, vp_prev]
        aliases = {11: 1, 12: 2}
    return pl.pallas_call(
        _cm_mix_kernel,
        out_shape=(jax.ShapeDtypeStruct((t, d), F32),
                   jax.ShapeDtypeStruct((n_slots, t - n_prompt, width), F32),
                   jax.ShapeDtypeStruct((n_slots, 2 * tm, width), F32)),
        grid=(t // tm,),
        in_specs=[
            pl.BlockSpec((tm, width), lambda i: (i, 0)),
            pl.BlockSpec((tm, width), lambda i: (i, 1)),
            pl.BlockSpec((tm, d), lambda i: (i, 0)),
            pl.BlockSpec((1, ng, 1, d), lambda i: (layer, i, 0, 2)),
            pl.BlockSpec((1, CM_GROUPS, CM_CHUNK, CM_CHUNK), lambda i: (kind(i), 0, 0, 0)),
            pl.BlockSpec((1, CM_CHUNK, width), lambda i: (kind(i), 0, 0)),
            pl.BlockSpec((1, width), lambda i: (0, 0)),
            pl.BlockSpec((1, width), lambda i: (0, 0)),
            pl.BlockSpec((None, width, d), lambda i: (slot, 0, 0), pipeline_mode=pl.Buffered(1)),
            pl.BlockSpec((1, d), lambda i: (0, 0)),
            pl.BlockSpec((1, d), lambda i: (0, 0)),
        ] + extra_specs,
        out_specs=(pl.BlockSpec((tm, d), lambda i: (i, 0)),
                   pl.BlockSpec((1, tm, width), lambda i: (slot, jnp.maximum(i - n_prompt_tiles, 0), 0)),
                   pl.BlockSpec((1, tm, width), lambda i: (slot, kind(i), 0))),
        scratch_shapes=[pltpu.VMEM((tm, width), BF16)],
        input_output_aliases=aliases,
        compiler_params=_cp(48, ("arbitrary",)),
        name="cm_mix",
    )(zz, zz, x, mods, mx, bias, lvg, lvb, wo, g, b, *extra_args)


def _split3(a):
    hi = a.astype(BF16)
    return hi, (a - hi.astype(F32)).astype(BF16)


def _expert_max(a):
    return jnp.max(jnp.max(a, axis=0, keepdims=True), axis=1, keepdims=True)


def _route_kernel(x_ref, sh_ref, sc_ref, wr_ref, br_ref, xm_ref, cw_ref, mk_ref, cnt_ref):
    i = pl.program_id(0)
    xm = _modulate(x_ref[...], sh_ref, sc_ref)
    _store_token_tiles(xm_ref, _pack_halves(xm))
    xh, xl = _split3(xm)
    wh, wl = _split3(wr_ref[...])
    nt = lambda a, b: lax.dot_general(a, b, (((1,), (1,)), ((), ())), preferred_element_type=F32)
    scores = jax.nn.sigmoid(nt(wh, xh) + (nt(wl, xh) + nt(wh, xl)))
    sel = scores + br_ref[...]
    ne, tm = sel.shape
    ng, per = N_EXPERT_GROUPS, ne // N_EXPERT_GROUPS
    sel3 = sel.reshape(ng, per, tm)
    neg = -jnp.inf
    within = lax.broadcasted_iota(I32, (ng, per, tm), 1)
    m1 = jnp.max(sel3, axis=1, keepdims=True)
    i1 = jnp.min(jnp.where(sel3 == m1, within, per), axis=1, keepdims=True)
    m2 = jnp.max(jnp.where(within == i1, neg, sel3), axis=1, keepdims=True)
    gs = m1 + m2
    gidx = lax.broadcasted_iota(I32, (ng, 1, tm), 0)
    rank = jnp.zeros((ng, 1, tm), I32)
    for g in range(ng):
        row = gs[g:g + 1]
        ahead = jnp.logical_or(row > gs, jnp.logical_and(row == gs, g < gidx))
        rank = rank + ahead.astype(I32)
    allowed = (rank < TOPK_GROUPS).astype(F32)
    ms = jnp.where(allowed > 0.0, sel3, neg)
    eidx = lax.broadcasted_iota(I32, (ng, per, tm), 0) * per + within
    chosen = jnp.zeros((ng, per, tm), F32)
    for _ in range(TOP_K):
        best = _expert_max(ms)
        first = -_expert_max(-jnp.where(ms == best, eidx, ne).astype(F32))
        pick = eidx.astype(F32) == first
        chosen = jnp.where(pick, 1.0, chosen)
        ms = jnp.where(pick, neg, ms)
    chosen = chosen.reshape(ne, tm)
    tw = chosen * scores
    cw_ref[...] = tw / jnp.sum(tw, axis=0, keepdims=True) * ROUTED_SCALE
    mk_ref[...] = chosen.astype(BF16)

    @pl.when(i == 0)
    def _():
        cnt_ref[...] = jnp.zeros_like(cnt_ref)

    cnt_ref[...] += jnp.sum(chosen, axis=1, keepdims=True)


def _route(x, mods, layer, wr_t, br_col):
    t, d = x.shape
    ne = wr_t.shape[0]
    tm = 256
    ng = tm // GROUP
    return pl.pallas_call(
        _route_kernel,
        out_shape=(jax.ShapeDtypeStruct((t * TOKEN_ROWS, LANES), U32), jax.ShapeDtypeStruct((ne, t), F32),
                   jax.ShapeDtypeStruct((ne, t), BF16), jax.ShapeDtypeStruct((ne, 1), F32)),
        grid=(t // tm,),
        in_specs=[
            pl.BlockSpec((tm, d), lambda i: (i, 0)),
            pl.BlockSpec((1, ng, 1, d), lambda i: (layer, i, 0, 0)),
            pl.BlockSpec((1, ng, 1, d), lambda i: (layer, i, 0, 1)),
            pl.BlockSpec((ne, d), lambda i: (0, 0)),
            pl.BlockSpec((ne, 1), lambda i: (0, 0)),
        ],
        out_specs=(pl.BlockSpec((tm * TOKEN_ROWS, LANES), lambda i: (i, 0)),
                   pl.BlockSpec((ne, tm), lambda i: (0, i)),
                   pl.BlockSpec((ne, tm), lambda i: (0, i)), pl.BlockSpec((ne, 1), lambda i: (0, 0))),
        compiler_params=_cp(32, ("arbitrary",)),
        name="router",
    )(x, mods, mods, wr_t, br_col)


def _slots_kernel(mk_ref, cw_ref, start_ref, before_ref, lower_ref, pos_ref, w_ref, carry_ref):
    i = pl.program_id(0)

    @pl.when(i == 0)
    def _():
        carry_ref[...] = jnp.zeros_like(carry_ref)

    mk = mk_ref[...]
    rank = jnp.dot(mk, before_ref[...], preferred_element_type=F32)
    pos = start_ref[...] + carry_ref[...] + rank
    order = jnp.dot(lower_ref[...], mk, preferred_element_type=F32)
    chosen = mk > 0
    cw = cw_ref[...]
    rows_p, rows_w = [], []
    for k in range(TOP_K):
        pick = jnp.logical_and(chosen, order == k)
        rows_p.append(jnp.sum(jnp.where(pick, pos, 0.0), axis=0, keepdims=True))
        rows_w.append(jnp.sum(jnp.where(pick, cw, 0.0), axis=0, keepdims=True))
    pos_ref[...] = jnp.concatenate(rows_p, axis=0).astype(I32)
    w_ref[...] = jnp.concatenate(rows_w, axis=0)
    carry_ref[...] += jnp.sum(mk.astype(F32), axis=1, keepdims=True)


def _slots(mk, cw, start_col):
    ne, t = mk.shape
    tm = 256
    r = jnp.arange(tm)
    before = (r[:, None] < r[None, :]).astype(BF16)
    return pl.pallas_call(
        _slots_kernel,
        out_shape=(jax.ShapeDtypeStruct((TOP_K, t), I32), jax.ShapeDtypeStruct((TOP_K, t), F32)),
        grid=(t // tm,),
        in_specs=[
            pl.BlockSpec((ne, tm), lambda i: (0, i)),
            pl.BlockSpec((ne, tm), lambda i: (0, i)),
            pl.BlockSpec((ne, 1), lambda i: (0, 0)),
            pl.BlockSpec((tm, tm), lambda i: (0, 0)),
            pl.BlockSpec((ne, ne), lambda i: (0, 0)),
        ],
        out_specs=(pl.BlockSpec((TOP_K, tm), lambda i: (0, i)), pl.BlockSpec((TOP_K, tm), lambda i: (0, i))),
        scratch_shapes=[pltpu.VMEM((ne, 1), F32)],
        compiler_params=_cp(32, ("arbitrary",)),
        name="plan",
    )(mk, cw, start_col, before, _strict_lower(ne))


def _dispatch_kernel(fill_ref, pad_ref, pos_ref, xm_ref, wsg_ref, wsu_ref, wsd_ref, xs_ref, sh_ref,
                     zeros_ref, sem):
    i = pl.program_id(0)
    tm = xm_ref.shape[0] // TOKEN_ROWS
    ne = fill_ref.shape[0]
    pad_bits = (zeros_ref.shape[0] // TOKEN_ROWS).bit_length() - 1

    def row_copy(r, k):
        return pltpu.make_async_copy(_token_tile(xm_ref, r), _token_tile(xs_ref, pos_ref[r * TOP_K + k]), sem)

    def fill(e, wait):
        first, n = fill_ref[e], pad_ref[e]
        for bit in range(pad_bits):
            size = (1 << bit) * TOKEN_ROWS
            done = (n >> (bit + 1)) << (bit + 1)
            dst = pl.multiple_of((first + done) * TOKEN_ROWS, TOKEN_ROWS)
            piece = pltpu.make_async_copy(zeros_ref.at[pl.ds(0, size)], xs_ref.at[pl.ds(dst, size)], sem)

            @pl.when(((n >> bit) & 1) == 1)
            def _(piece=piece):
                if wait:
                    piece.wait()
                else:
                    piece.start()

    @pl.when(i == 0)
    def _():
        zeros_ref[...] = jnp.zeros_like(zeros_ref)

        def start(e, c):
            fill(e, False)
            return c

        def wait(e, c):
            fill(e, True)
            return c

        lax.fori_loop(0, ne, start, 0)
        lax.fori_loop(0, ne, wait, 0)

    def start(r, c):
        for k in range(TOP_K):
            row_copy(r, k).start(priority=k % 2)
        return c

    def wait(r, c):
        for k in range(TOP_K):
            row_copy(r, k).wait()
        return c

    lax.fori_loop(0, tm // 2, start, 0)
    lo, hi = _unpack_halves(_load_token_tiles(xm_ref))
    xb = jnp.concatenate([lo, hi], axis=1).astype(BF16)
    hg = jnp.dot(xb, wsg_ref[...], preferred_element_type=F32)
    hu = jnp.dot(xb, wsu_ref[...], preferred_element_type=F32)
    f = jnp.dot((jax.nn.silu(hg) * hu).astype(BF16), wsd_ref[...], preferred_element_type=F32)
    sh_ref[...] = f.astype(sh_ref.dtype)
    lax.fori_loop(tm // 2, tm, start, 0)
    lax.fori_loop(0, tm, wait, 0)


def _dispatch(fill, pad, pos_flat, xm, wsg, wsu, wsd, layer, n_slots):
    t = xm.shape[0] // TOKEN_ROWS
    d, fs = wsg.shape[1], wsg.shape[2]
    tm = 256
    return pl.pallas_call(
        _dispatch_kernel,
        out_shape=(jax.ShapeDtypeStruct((n_slots * TOKEN_ROWS, LANES), U32),
                   jax.ShapeDtypeStruct((t, d), BF16)),
        grid_spec=pltpu.PrefetchScalarGridSpec(
            num_scalar_prefetch=2,
            grid=(t // tm,),
            in_specs=[
                pl.BlockSpec((tm * TOP_K,), lambda i, fill, pad: (i,), memory_space=pltpu.SMEM),
                pl.BlockSpec((tm * TOKEN_ROWS, LANES), lambda i, fill, pad: (i, 0)),
                pl.BlockSpec((None, d, fs), lambda i, fill, pad: (layer, 0, 0)),
                pl.BlockSpec((None, d, fs), lambda i, fill, pad: (layer, 0, 0)),
                pl.BlockSpec((None, fs, d), lambda i, fill, pad: (layer, 0, 0)),
            ],
            out_specs=(pl.BlockSpec(memory_space=pl.ANY),
                       pl.BlockSpec((tm, d), lambda i, fill, pad: (i, 0))),
            scratch_shapes=[pltpu.VMEM((EXPERT_TILE * TOKEN_ROWS, LANES), U32), pltpu.SemaphoreType.DMA(())],
        ),
        compiler_params=_cp(40, ("arbitrary",)),
        name="dispatch",
    )(fill, pad, pos_flat, xm, wsg, wsu, wsd)


def _gmm_kernel(te_ref, nt_ref, nxt_ref, slot_ref, xs_ref, wg_ref, wu_ref, wd_ref, ys_ref,
                fg_ref, fu_ref, fd_ref, wgb_ref, wub_ref, wdb_ref, sems, *, layer):
    i = pl.program_id(0)
    e = te_ref[i]
    s = slot_ref[i]
    prev = te_ref[jnp.maximum(i - 1, 0)]

    def fetch(expert, slot):
        return (pltpu.make_async_copy(wg_ref.at[layer, expert], fg_ref.at[slot], sems.at[slot]),
                pltpu.make_async_copy(wu_ref.at[layer, expert], fu_ref.at[slot], sems.at[slot]),
                pltpu.make_async_copy(wd_ref.at[layer, expert], fd_ref.at[slot], sems.at[slot]))

    @pl.when(i == 0)
    def _():
        for c in fetch(e, s):
            c.start()

    for slot in range(2):
        @pl.when(jnp.logical_and(jnp.logical_or(i == 0, e != prev), s == slot))
        def _(slot=slot):
            for c in fetch(e, slot):
                c.wait()
            wgb_ref[...] = fg_ref[slot].astype(BF16)
            wub_ref[...] = fu_ref[slot].astype(BF16)
            wdb_ref[...] = fd_ref[slot].astype(BF16)

            @pl.when(nxt_ref[i] >= 0)
            def _():
                for c in fetch(nxt_ref[i], 1 - slot):
                    c.start()

    @pl.when(i < nt_ref[0])
    def _():
        lo, hi = _unpack_halves(_load_token_tiles(xs_ref))
        x = jnp.concatenate([lo, hi], axis=1).astype(BF16)
        hg = jnp.dot(x, wgb_ref[...], preferred_element_type=F32)
        hu = jnp.dot(x, wub_ref[...], preferred_element_type=F32)
        hid = (jax.nn.silu(hg) * hu).astype(BF16)
        y = jnp.dot(hid, wdb_ref[...], preferred_element_type=F32)
        _store_token_tiles(ys_ref, _pack_halves(y))


def _gmm(tile_expert, n_tiles_used, tile_next, tile_slot, xs, wg, wu, wd, layer, max_tiles):
    d, f = wg.shape[2], wg.shape[3]
    tm = EXPERT_TILE
    row = lambda i, te, nt, nx, sl: (jnp.minimum(i, nt[0] - 1), 0)
    return pl.pallas_call(
        functools.partial(_gmm_kernel, layer=layer),
        out_shape=jax.ShapeDtypeStruct(xs.shape, U32),
        grid_spec=pltpu.PrefetchScalarGridSpec(
            num_scalar_prefetch=4,
            grid=(max_tiles,),
            in_specs=[
                pl.BlockSpec((tm * TOKEN_ROWS, LANES), row),
                pl.BlockSpec(memory_space=pl.ANY),
                pl.BlockSpec(memory_space=pl.ANY),
                pl.BlockSpec(memory_space=pl.ANY),
            ],
            out_specs=pl.BlockSpec((tm * TOKEN_ROWS, LANES), row),
            scratch_shapes=[pltpu.VMEM((2, d, f), F32), pltpu.VMEM((2, d, f), F32), pltpu.VMEM((2, f, d), F32),
                            pltpu.VMEM((d, f), BF16), pltpu.VMEM((d, f), BF16), pltpu.VMEM((f, d), BF16),
                            pltpu.SemaphoreType.DMA((2,))],
        ),
        compiler_params=_cp(56, ("arbitrary",)),
        name="gmm",
    )(tile_expert, n_tiles_used, tile_next, tile_slot, xs, wg, wu, wd)


def _combine_kernel(pos_ref, nxt_ref, w_ref, sh_ref, x_ref, gate_ref, g_ref, b_ref, ys_ref, o_ref,
                    buf_ref, sems):
    i = pl.program_id(0)
    n = pl.num_programs(0)
    tm, d = x_ref.shape
    half = d // 2
    slot = i % 2

    def copy(p_ref, s, r, k):
        return pltpu.make_async_copy(_token_tile(ys_ref, p_ref[r * TOP_K + k]),
                                     _token_tile(buf_ref.at[s, k], r), sems.at[s])

    def gather(p_ref, s):
        def start(r, c):
            for k in range(TOP_K):
                copy(p_ref, s, r, k).start(priority=k % 2)
            return c
        lax.fori_loop(0, tm, start, 0)

    @pl.when(i == 0)
    def _():
        gather(pos_ref, 0)

    for s in range(2):
        @pl.when(jnp.logical_and(i + 1 < n, slot == 1 - s))
        def _(s=s):
            gather(nxt_ref, s)

    def wait(r, c):
        for k in range(TOP_K):
            copy(pos_ref, slot, r, k).wait()
        return c

    lax.fori_loop(0, tm, wait, 0)
    f = sh_ref[...].astype(F32)
    w = w_ref[...]
    wk = [jnp.broadcast_to(w[:, k:k + 1], (tm, LANES)) for k in range(TOP_K)]
    los, his = [], []
    for c in range(TOKEN_ROWS):
        acc_lo = f[:, c * LANES:(c + 1) * LANES]
        acc_hi = f[:, half + c * LANES:half + (c + 1) * LANES]
        for k in range(TOP_K):
            lo, hi = _unpack_halves(buf_ref[slot, k, pl.ds(c, tm, stride=TOKEN_ROWS), :])
            acc_lo = acc_lo + wk[k] * lo
            acc_hi = acc_hi + wk[k] * hi
        los.append(acc_lo)
        his.append(acc_hi)
    f = jnp.concatenate(los + his, axis=1)
    o_ref[...] = _residual_ln(x_ref[...], f, gate_ref, g_ref, b_ref)


def _combine(pos_flat, w, shared, x, mods, layer, g, b, ys, row0, n_rows):
    d = x.shape[1]
    tm = 128
    ng = tm // GROUP
    off = row0 // tm
    last = n_rows // tm - 1
    return pl.pallas_call(
        _combine_kernel,
        out_shape=jax.ShapeDtypeStruct((n_rows, d), F32),
        grid=(n_rows // tm,),
        in_specs=[
            pl.BlockSpec((tm * TOP_K,), lambda i: (i + off,), memory_space=pltpu.SMEM),
            pl.BlockSpec((tm * TOP_K,), lambda i: (jnp.minimum(i + 1, last) + off,), memory_space=pltpu.SMEM),
            pl.BlockSpec((tm, TOP_K), lambda i: (i + off, 0)),
            pl.BlockSpec((tm, d), lambda i: (i + off, 0)),
            pl.BlockSpec((tm, d), lambda i: (i + off, 0)),
            pl.BlockSpec((1, ng, 1, d), lambda i: (layer, i + off, 0, 2)),
            pl.BlockSpec((1, d), lambda i: (0, 0)),
            pl.BlockSpec((1, d), lambda i: (0, 0)),
            pl.BlockSpec(memory_space=pl.ANY),
        ],
        out_specs=pl.BlockSpec((tm, d), lambda i: (i, 0)),
        scratch_shapes=[pltpu.VMEM((2, TOP_K, tm * TOKEN_ROWS, LANES), U32), pltpu.SemaphoreType.DMA((2,))],
        compiler_params=_cp(48, ("arbitrary",)),
        name="combine",
    )(pos_flat, pos_flat, w, shared, x, mods, g, b, ys)


def _moe(x, mods, layer, wr, br, wg, wu, wd, wsg, wsu, wsd, g, b, splits):
    t = x.shape[0]
    ne = wr.shape[1]
    tile = EXPERT_TILE
    max_tiles = (t * TOP_K) // tile + ne
    xm, cw, mk, cnt = _route(x, mods, layer, wr.T, br.reshape(ne, 1))
    ids = jnp.arange(ne, dtype=I32)
    upto = ids[None, :] <= ids[:, None]
    cnt = cnt[:, 0].astype(I32)
    tiles = (cnt + tile - 1) // tile
    ends = jnp.sum(jnp.where(upto, tiles[None, :], 0), axis=1)
    start = (ends - tiles) * tile
    n_used = ends[-1:]
    tile_ids = jnp.minimum(jnp.arange(max_tiles, dtype=I32), n_used[0] - 1)
    tile_expert = jnp.sum((ends[None, :] <= tile_ids[:, None]).astype(I32), axis=1)
    used = tiles > 0
    later_used = jnp.logical_and(ids[None, :] > ids[:, None], used[None, :])
    next_used = jnp.min(jnp.where(later_used, ids[None, :], ne), axis=1)
    next_used = jnp.where(next_used < ne, next_used, -1).astype(I32)
    slot = ((jnp.sum(jnp.logical_and(upto, used[None, :]).astype(I32), axis=1) - 1) % 2).astype(I32)
    pos, w = _slots(mk, cw, start.astype(F32)[:, None])
    pos_flat = pos.T.reshape(-1)
    w = w.T
    xs, shared = _dispatch(start + cnt, tiles * tile - cnt, pos_flat, xm, wsg, wsu, wsd, layer,
                           max_tiles * tile)
    of_tile = (tile_expert[:, None] == ids[None, :]).astype(I32)
    tile_next = jnp.sum(of_tile * next_used[None, :], axis=1)
    tile_slot = jnp.sum(of_tile * slot[None, :], axis=1)
    ys = _gmm(tile_expert, n_used.astype(I32), tile_next, tile_slot, xs, wg, wu, wd, layer, max_tiles)
    return [_combine(pos_flat, w, shared, x, mods, layer, g, b, ys, row0, n_rows)
            for row0, n_rows in splits]


def _expand_mods(mod, n_prompt_groups, n_streams):
    depth, _, n = mod.shape
    p = jnp.broadcast_to(mod[:, 0:1], (depth, n_prompt_groups, n))
    return jnp.concatenate([p, mod[:, 1:1 + n_streams]], axis=1)[:, :, None, :]


def kernel(x_prompt, x_sample, c_prompt, c_sample, cache_k, cache_v, w_ada_mix, b_ada_mix, w_ada_ffn, b_ada_ffn, ln_mix_g, ln_mix_b, ln_ffn_g, ln_ffn_b, w_qkv, w_o, w_uv, b_uv, ln_v_g, ln_v_b, w_s, b_s, w_cm_out, w_router, b_router, w_gate, w_up, w_down, ws_gate, ws_up, ws_down):
    bp, sp, d = x_prompt.shape
    bs, t_new, _ = x_sample.shape
    assert bp == 1 and t_new == GROUP and sp % ATT_BLOCK == 0 and d == N_HEADS * HEAD_DIM
    n_prompt = bp * sp
    n_sample = bs * t_new
    width = w_uv.shape[2] // 2
    past = cache_k.shape[2]

    x = jnp.concatenate([x_prompt.reshape(n_prompt, d), x_sample.reshape(n_sample, d)], axis=0)

    rows = 8 * ((1 + bs + 7) // 8)
    c_all = jnp.zeros((rows, d), F32).at[0:1].set(c_prompt).at[1:1 + bs].set(c_sample)
    mods_mix = _expand_mods(_ada(c_all, w_ada_mix, b_ada_mix), n_prompt // GROUP, bs)
    mods_ffn = _expand_mods(_ada(c_all, w_ada_ffn, b_ada_ffn), n_prompt // GROUP, bs)

    cpos = jnp.arange(CM_CHUNK)
    cmask = (cpos[None, :] // CHUNK) <= (cpos[:, None] // CHUNK)
    pair = (cpos[None, :] // t_new) == (cpos[:, None] // t_new)
    fold = cpos % t_new
    smask = (fold[None, :] // CHUNK) <= (fold[:, None] // CHUNK)
    ws_prompt = jnp.where(cmask, w_s, 0.0)
    ws_sample = jnp.where(jnp.logical_and(pair, smask), w_s[:, :, fold][:, :, :, fold], 0.0)
    mix = jnp.stack([ws_prompt, ws_sample], axis=1).astype(BF16)
    gw = width // CM_GROUPS
    bias_p = jnp.repeat(jnp.swapaxes(b_s, 1, 2), gw, axis=2)
    mix_bias = jnp.stack([bias_p, bias_p[:, fold]], axis=1)

    ck = cache_k.reshape(cache_k.shape[0], bs, past * N_HEADS, HEAD_DIM)
    cv = cache_v.reshape(cache_v.shape[0], bs, past * N_HEADS, HEAD_DIM)

    n_sb = (DEPTH + 1) // 2
    n_cm = DEPTH // 2
    kp = vp = ksm = vsm = cms = cmp_ = None
    wq, wo = w_qkv.astype(BF16), w_o.astype(BF16)
    wuv, wcm = w_uv.astype(BF16), w_cm_out.astype(BF16)
    wsg, wsu, wsd = ws_gate.astype(BF16), ws_up.astype(BF16), ws_down.astype(BF16)
    buv = b_uv[:, None, :]
    for i in range(DEPTH):
        j = i // 2
        if i % 2 == 0:
            qkv_p, kp, vp = _qkv(x, mods_mix, i, wq, 0, n_prompt, j, n_sb, kp, vp)
            qkv_s, ksm, vsm = _qkv(x, mods_mix, i, wq, n_prompt, n_sample, j, n_sb, ksm, vsm)
            o_p = _attn_prompt(qkv_p, n_prompt)
            o_s, cs = _attn_sample(qkv_s, ck, cv, j, 0, bs, t_new)
            o_s = lax.cond(
                jnp.max(cs) > STICK_DONE,
                lambda: _attn_sample_rest(qkv_s, o_s, cs, ck, cv, j, 0, bs, t_new),
                lambda: o_s)
            x = _proj_ln(o_p, o_s, x, mods_mix, i, wo, j, ln_mix_g[i][None], ln_mix_b[i][None])
        else:
            zz = _modmm(x, mods_mix, i, wuv, buv, j, BF16, True)
            x, cms, cmp_ = _cm_mix(zz, x, mods_mix, i, mix[j], mix_bias[j], ln_v_g[j][None], ln_v_b[j][None],
                                   wcm, ln_mix_g[i][None], ln_mix_b[i][None],
                                   n_prompt, j, n_cm, cms, cmp_)
        last = i == DEPTH - 1
        splits = [(0, n_prompt), (n_prompt, n_sample)] if last else [(0, n_prompt + n_sample)]
        outs = _moe(x, mods_ffn, i, w_router[i], b_router[i][None], w_gate, w_up, w_down,
                    wsg, wsu, wsd, ln_ffn_g[i][None], ln_ffn_b[i][None], splits)
        x = outs[0]

    y_prompt, y_sample = outs
    return (
        y_prompt.reshape(bp, sp, d),
        y_sample.reshape(bs, t_new, d),
        kp.reshape(n_sb, bp, sp, N_HEADS, HEAD_DIM),
        vp.reshape(n_sb, bp, sp, N_HEADS, HEAD_DIM),
        ksm.reshape(n_sb, bs, t_new, N_HEADS, HEAD_DIM),
        vsm.reshape(n_sb, bs, t_new, N_HEADS, HEAD_DIM),
        cmp_[:, :CM_CHUNK].reshape(n_cm, bp, CM_CHUNK, width),
        cms.reshape(n_cm, bs, t_new, width),
    )
```

```python
import functools

import jax
import jax.numpy as jnp
from jax import lax
from jax.experimental import pallas as pl
from jax.experimental.pallas import tpu as pltpu

F32 = jnp.float32
BF16 = jnp.bfloat16
I32 = jnp.int32

DEPTH = 4
N_HEADS = 16
HEAD_DIM = 128
CHUNK = 64
CM_CHUNK = 128
CM_GROUPS = 16
N_EXPERTS = 64
N_EXPERT_GROUPS = 8
TOPK_GROUPS = 4
TOP_K = 8
ROUTED_SCALE = 2.5
ALPHA = (2 * DEPTH) ** 0.25
LN_EPS = 1e-5

GROUP = 64
MIB = 1024 * 1024
STICK_DONE = -104.0
ATT_BLOCK = 256
EXPERT_TILE = 256


def _cp(vmem_mib, sem):
    return pltpu.CompilerParams(dimension_semantics=sem, vmem_limit_bytes=vmem_mib * MIB)


def _ln(y, g, b):
    mu = jnp.mean(y, axis=-1, keepdims=True)
    yc = y - mu
    var = jnp.mean(yc * yc, axis=-1, keepdims=True)
    return yc * lax.rsqrt(var + LN_EPS) * g + b


def _modulate(x, sh_ref, sc_ref):
    tm, d = x.shape
    g = sh_ref.shape[1]
    xg = x.reshape(g, tm // g, d)
    return (xg * sc_ref[0] + sh_ref[0]).reshape(tm, d)


def _residual_ln(x, f, gate_ref, g_ref, b_ref):
    tm, d = x.shape
    g = gate_ref.shape[1]
    y = ALPHA * x + (gate_ref[0] * f.reshape(g, tm // g, d)).reshape(tm, d)
    return _ln(y, g_ref[...], b_ref[...])


TOKEN_ROWS = 8
LANES = 128
U32 = jnp.uint32


def _pack_halves(y):
    n = y.shape[1] // 2
    lo = lax.bitcast_convert_type(y[:, :n].astype(BF16).astype(F32), U32)
    hi = lax.bitcast_convert_type(y[:, n:].astype(BF16).astype(F32), U32)
    return hi | (lo >> 16)


def _unpack_halves(w):
    lo = lax.bitcast_convert_type(w << 16, F32)
    hi = lax.bitcast_convert_type(w & jnp.uint32(0xFFFF0000), F32)
    return lo, hi


def _store_token_tiles(ref, words):
    tm = words.shape[0]
    for c in range(TOKEN_ROWS):
        ref[pl.ds(c, tm, stride=TOKEN_ROWS), :] = words[:, c * LANES:(c + 1) * LANES]


def _load_token_tiles(ref):
    tm = ref.shape[0] // TOKEN_ROWS
    return jnp.concatenate([ref[pl.ds(c, tm, stride=TOKEN_ROWS), :] for c in range(TOKEN_ROWS)], axis=1)


def _token_tile(ref, i):
    return ref.at[pl.ds(pl.multiple_of(i * TOKEN_ROWS, TOKEN_ROWS), TOKEN_ROWS)]


def _ada_kernel(c_ref, w_ref, b_ref, o_ref, *, d_model, tn):
    j = pl.program_id(1)
    a = jax.nn.silu(c_ref[...]).astype(BF16)
    acc = jnp.dot(a, w_ref[0].astype(BF16), preferred_element_type=F32) + b_ref[0]
    o_ref[0] = acc + jnp.where(j * tn >= d_model, 1.0, 0.0).astype(F32)


def _ada(c_all, w, b):
    depth, d, n = w.shape
    r = c_all.shape[0]
    tn = 1024
    return pl.pallas_call(
        functools.partial(_ada_kernel, d_model=d, tn=tn),
        out_shape=jax.ShapeDtypeStruct((depth, r, n), F32),
        grid=(depth, n // tn),
        in_specs=[
            pl.BlockSpec((r, d), lambda l, j: (0, 0)),
            pl.BlockSpec((1, d, tn), lambda l, j: (l, 0, j)),
            pl.BlockSpec((1, 1, tn), lambda l, j: (l, 0, j)),
        ],
        out_specs=pl.BlockSpec((1, r, tn), lambda l, j: (l, 0, j)),
        compiler_params=_cp(40, ("parallel", "parallel")),
        name="ada",
    )(c_all, w, b.reshape(depth, 1, n))


def _modmm_kernel(x_ref, sh_ref, sc_ref, w_ref, b_ref, o_ref, *, gelu):
    h = _modulate(x_ref[...], sh_ref, sc_ref).astype(BF16)
    acc = jnp.dot(h, w_ref[...], preferred_element_type=F32) + b_ref[...]
    if gelu:
        acc = jax.nn.gelu(acc)
    o_ref[...] = acc.astype(o_ref.dtype)


def _modmm(x, mods, layer, w, b, wslot, out_dtype, gelu):
    t, d = x.shape
    n = w.shape[2]
    tm, tn = 512, 2048
    g = tm // GROUP
    return pl.pallas_call(
        functools.partial(_modmm_kernel, gelu=gelu),
        out_shape=jax.ShapeDtypeStruct((t, n), out_dtype),
        grid=(n // tn, t // tm),
        in_specs=[
            pl.BlockSpec((tm, d), lambda j, i: (i, 0)),
            pl.BlockSpec((1, g, 1, d), lambda j, i: (layer, i, 0, 0)),
            pl.BlockSpec((1, g, 1, d), lambda j, i: (layer, i, 0, 1)),
            pl.BlockSpec((None, d, tn), lambda j, i: (wslot, 0, j)),
            pl.BlockSpec((None, 1, tn), lambda j, i: (wslot, 0, j)),
        ],
        out_specs=pl.BlockSpec((tm, tn), lambda j, i: (i, j)),
        compiler_params=_cp(48, ("parallel", "parallel")),
        name="modmm",
    )(x, mods, mods, w, b)


def _qkv_kernel(x_ref, sh_ref, sc_ref, w_ref, *rest):
    qkv_ref, kf_ref, vf_ref = rest[-3:]
    h = _modulate(x_ref[...], sh_ref, sc_ref).astype(BF16)
    acc = jnp.dot(h, w_ref[...], preferred_element_type=F32)
    qkv_ref[...] = acc.astype(BF16)
    tm = acc.shape[0]
    d = acc.shape[1] // 3
    for hd in range(N_HEADS):
        cols = slice(hd * HEAD_DIM, (hd + 1) * HEAD_DIM)
        kf_ref[0, pl.ds(hd, tm, stride=N_HEADS), :] = acc[:, d:2 * d][:, cols]
        vf_ref[0, pl.ds(hd, tm, stride=N_HEADS), :] = acc[:, 2 * d:][:, cols]


def _qkv(x, mods, layer, w, row0, n_rows, slot, n_slots, kf_prev, vf_prev):
    d = x.shape[1]
    tm = 256
    g = tm // GROUP
    off = row0 // tm
    flat = (n_slots, n_rows * N_HEADS, HEAD_DIM)
    in_specs = [
        pl.BlockSpec((tm, d), lambda i: (i + off, 0)),
        pl.BlockSpec((1, g, 1, d), lambda i: (layer, i + off, 0, 0)),
        pl.BlockSpec((1, g, 1, d), lambda i: (layer, i + off, 0, 1)),
        pl.BlockSpec((None, d, 3 * d), lambda i: (slot, 0, 0), pipeline_mode=pl.Buffered(1)),
    ]
    args = [x, mods, mods, w]
    aliases = {}
    if kf_prev is not None:
        in_specs += [pl.BlockSpec(memory_space=pl.ANY), pl.BlockSpec(memory_space=pl.ANY)]
        args += [kf_prev, vf_prev]
        aliases = {4: 1, 5: 2}
    return pl.pallas_call(
        _qkv_kernel,
        out_shape=(jax.ShapeDtypeStruct((n_rows, 3 * d), BF16),
                   jax.ShapeDtypeStruct(flat, F32), jax.ShapeDtypeStruct(flat, F32)),
        grid=(n_rows // tm,),
        in_specs=in_specs,
        out_specs=(pl.BlockSpec((tm, 3 * d), lambda i: (i, 0)),
                   pl.BlockSpec((1, tm * N_HEADS, HEAD_DIM), lambda i: (slot, i, 0)),
                   pl.BlockSpec((1, tm * N_HEADS, HEAD_DIM), lambda i: (slot, i, 0))),
        input_output_aliases=aliases,
        compiler_params=_cp(56, ("parallel",)),
        name="qkv",
    )(*args)


def _sb_block(q, k, v, csum, u, mask):
    z = lax.dot_general(q, k, (((1,), (1,)), ((), ())), preferred_element_type=F32)
    z = z * (HEAD_DIM ** -0.5)
    t = jnp.log(1.0 + jnp.exp(-jnp.abs(z)))
    log_beta = -(jnp.maximum(-z, 0.0) + t)
    l1 = log_beta - z
    if mask is not None:
        l1 = jnp.where(mask, l1, 0.0)
    hi = l1.astype(BF16)
    lo = (l1 - hi.astype(F32)).astype(BF16)
    s = jnp.dot(hi, u, preferred_element_type=F32) + jnp.dot(lo, u, preferred_element_type=F32)
    a = jnp.exp(log_beta + s + csum)
    if mask is not None:
        a = jnp.where(mask, a, 0.0)
    o = jnp.dot(a.astype(BF16), v, preferred_element_type=F32)
    return o, csum + s[:, :1] + l1[:, :1]


def _causal_mask(n):
    row = lax.broadcasted_iota(I32, (n, n), 0)
    col = lax.broadcasted_iota(I32, (n, n), 1)
    return col < row


def _attn_prompt_kernel(q_ref, k_ref, v_ref, u_ref, o_ref):
    i = pl.program_id(1)
    bq = q_ref.shape[0]
    heads = q_ref.shape[1] // HEAD_DIM
    u = u_ref[...]
    mask = _causal_mask(bq)

    def kv(b, cols):
        start = pl.multiple_of(b * bq, bq)
        return (k_ref[pl.ds(start, bq), cols].astype(BF16), v_ref[pl.ds(start, bq), cols].astype(BF16))

    prev = jnp.maximum(i - 1, 0)
    has_prev = i > 0
    qs, os, css = [], [], []
    for h in range(heads):
        cols = slice(h * HEAD_DIM, (h + 1) * HEAD_DIM)
        q = q_ref[:, cols].astype(BF16)
        k0, v0 = kv(i, cols)
        o, cs = _sb_block(q, k0, v0, jnp.zeros((bq, 1), F32), u, mask)
        k1, v1 = kv(prev, cols)
        do, cs1 = _sb_block(q, k1, v1, cs, u, None)
        qs.append(q)
        os.append(o + jnp.where(has_prev, do, 0.0))
        css.append(jnp.where(has_prev, cs1, cs))

    def cond(c):
        b, _, css = c
        live = jnp.max(css[0])
        for cs in css[1:]:
            live = jnp.maximum(live, jnp.max(cs))
        return jnp.logical_and(b >= 0, live > STICK_DONE)

    def body(c):
        b, os, css = c
        new_os, new_css = [], []
        for h in range(heads):
            cols = slice(h * HEAD_DIM, (h + 1) * HEAD_DIM)
            kb, vb = kv(b, cols)
            do, cs = _sb_block(qs[h], kb, vb, css[h], u, None)
            new_os.append(os[h] + do)
            new_css.append(cs)
        return b - 1, tuple(new_os), tuple(new_css)

    _, os, _ = lax.while_loop(cond, body, (i - 2, tuple(os), tuple(css)))
    for h in range(heads):
        o_ref[:, h * HEAD_DIM:(h + 1) * HEAD_DIM] = os[h].astype(o_ref.dtype)


def _strict_lower(n):
    r = jnp.arange(n)
    return (r[:, None] > r[None, :]).astype(BF16)


def _attn_prompt(qkv, n_prompt):
    d = N_HEADS * HEAD_DIM
    bq = ATT_BLOCK
    hp = 4
    groups = N_HEADS // hp
    wcols = hp * HEAD_DIM
    return pl.pallas_call(
        _attn_prompt_kernel,
        out_shape=jax.ShapeDtypeStruct((n_prompt, d), BF16),
        grid=(groups, n_prompt // bq),
        in_specs=[
            pl.BlockSpec((bq, wcols), lambda h, i: (i, h)),
            pl.BlockSpec((n_prompt, wcols), lambda h, i: (0, groups + h)),
            pl.BlockSpec((n_prompt, wcols), lambda h, i: (0, 2 * groups + h)),
            pl.BlockSpec((bq, bq), lambda h, i: (0, 0)),
        ],
        out_specs=pl.BlockSpec((bq, wcols), lambda h, i: (i, h)),
        compiler_params=_cp(48, ("parallel", "parallel")),
        name="attn_prompt",
    )(qkv, qkv, qkv, _strict_lower(bq))


def _attn_sample_kernel(q_ref, kn_ref, vn_ref, ck_ref, cv_ref, un_ref, uc_ref, o_ref, cs_ref):
    tq = q_ref.shape[0]
    bk = uc_ref.shape[0]
    mask = _causal_mask(tq)
    for h in range(N_HEADS):
        cols = slice(h * HEAD_DIM, (h + 1) * HEAD_DIM)
        q = q_ref[:, cols].astype(BF16)
        o, cs = _sb_block(q, kn_ref[:, cols].astype(BF16), vn_ref[:, cols].astype(BF16),
                          jnp.zeros((tq, 1), F32), un_ref[...], mask)
        kc = ck_ref[0, 0, pl.ds(h, bk, stride=N_HEADS), :].astype(BF16)
        vc = cv_ref[0, 0, pl.ds(h, bk, stride=N_HEADS), :].astype(BF16)
        do, cs = _sb_block(q, kc, vc, cs, uc_ref[...], None)
        o_ref[:, cols] = o + do
        cs_ref[:, cols] = jnp.broadcast_to(cs, (tq, HEAD_DIM))


def _attn_sample(qkv, cache_k, cache_v, layer, n_prompt, n_streams, t_new):
    d = N_HEADS * HEAD_DIM
    past = cache_k.shape[2] // N_HEADS
    bk = ATT_BLOCK
    pb = n_prompt // t_new
    last = past // bk - 1
    return pl.pallas_call(
        _attn_sample_kernel,
        out_shape=(jax.ShapeDtypeStruct((n_streams * t_new, d), F32),
                   jax.ShapeDtypeStruct((n_streams * t_new, d), F32)),
        grid=(n_streams,),
        in_specs=[
            pl.BlockSpec((t_new, d), lambda s: (pb + s, 0)),
            pl.BlockSpec((t_new, d), lambda s: (pb + s, 1)),
            pl.BlockSpec((t_new, d), lambda s: (pb + s, 2)),
            pl.BlockSpec((1, 1, bk * N_HEADS, HEAD_DIM), lambda s: (layer, s, last, 0)),
            pl.BlockSpec((1, 1, bk * N_HEADS, HEAD_DIM), lambda s: (layer, s, last, 0)),
            pl.BlockSpec((t_new, t_new), lambda s: (0, 0)),
            pl.BlockSpec((bk, bk), lambda s: (0, 0)),
        ],
        out_specs=(pl.BlockSpec((t_new, d), lambda s: (s, 0)),
                   pl.BlockSpec((t_new, d), lambda s: (s, 0))),
        compiler_params=_cp(32, ("parallel",)),
        name="attn_sample",
    )(qkv, qkv, qkv, cache_k, cache_v, _strict_lower(t_new), _strict_lower(bk))


def _attn_sample_rest_kernel(q_ref, oin_ref, csin_ref, ck_ref, cv_ref, u_ref, o_ref, cs_ref):
    b = pl.program_id(1)
    bk = u_ref.shape[0]

    @pl.when(b == 0)
    def _():
        o_ref[...] = oin_ref[...]
        cs_ref[...] = csin_ref[...]

    for h in range(N_HEADS):
        cols = slice(h * HEAD_DIM, (h + 1) * HEAD_DIM)
        cs = cs_ref[:, cols][:, :1]

        @pl.when(jnp.max(cs) > STICK_DONE)
        def _(h=h, cols=cols, cs=cs):
            q = q_ref[:, cols].astype(BF16)
            kc = ck_ref[0, 0, pl.ds(h, bk, stride=N_HEADS), :].astype(BF16)
            vc = cv_ref[0, 0, pl.ds(h, bk, stride=N_HEADS), :].astype(BF16)
            do, cs2 = _sb_block(q, kc, vc, cs, u_ref[...], None)
            o_ref[:, cols] += do
            cs_ref[:, cols] = jnp.broadcast_to(cs2, (cs2.shape[0], HEAD_DIM))


def _attn_sample_rest(qkv, o_part, cs_part, cache_k, cache_v, layer, n_prompt, n_streams, t_new):
    d = N_HEADS * HEAD_DIM
    past = cache_k.shape[2] // N_HEADS
    bk = ATT_BLOCK
    pb = n_prompt // t_new
    nb = past // bk - 1
    o, _ = pl.pallas_call(
        _attn_sample_rest_kernel,
        out_shape=(jax.ShapeDtypeStruct((n_streams * t_new, d), F32),
                   jax.ShapeDtypeStruct((n_streams * t_new, d), F32)),
        grid=(n_streams, nb),
        in_specs=[
            pl.BlockSpec((t_new, d), lambda s, b: (pb + s, 0)),
            pl.BlockSpec((t_new, d), lambda s, b: (s, 0)),
            pl.BlockSpec((t_new, d), lambda s, b: (s, 0)),
            pl.BlockSpec((1, 1, bk * N_HEADS, HEAD_DIM), lambda s, b: (layer, s, nb - 1 - b, 0)),
            pl.BlockSpec((1, 1, bk * N_HEADS, HEAD_DIM), lambda s, b: (layer, s, nb - 1 - b, 0)),
            pl.BlockSpec((bk, bk), lambda s, b: (0, 0)),
        ],
        out_specs=(pl.BlockSpec((t_new, d), lambda s, b: (s, 0)),
                   pl.BlockSpec((t_new, d), lambda s, b: (s, 0))),
        compiler_params=_cp(32, ("parallel", "arbitrary")),
        name="attn_sample_rest",
    )(qkv, o_part, cs_part, cache_k, cache_v, _strict_lower(bk))
    return o


def _proj_ln_kernel(ap_ref, as_ref, x_ref, gate_ref, w_ref, g_ref, b_ref, o_ref, *, prompt_tiles):
    a = jnp.where(pl.program_id(0) < prompt_tiles, ap_ref[...], as_ref[...].astype(BF16))
    f = jnp.dot(a, w_ref[...], preferred_element_type=F32)
    o_ref[...] = _residual_ln(x_ref[...], f, gate_ref, g_ref, b_ref)


def _proj_ln(a_prompt, a_sample, x, mods, layer, w, wslot, g, b):
    t, d = x.shape
    kdim = a_prompt.shape[1]
    tm = 256
    ng = tm // GROUP
    pt = a_prompt.shape[0] // tm
    return pl.pallas_call(
        functools.partial(_proj_ln_kernel, prompt_tiles=pt),
        out_shape=jax.ShapeDtypeStruct((t, d), F32),
        grid=(t // tm,),
        in_specs=[
            pl.BlockSpec((tm, kdim), lambda i: (jnp.minimum(i, pt - 1), 0)),
            pl.BlockSpec((tm, kdim), lambda i: (jnp.maximum(i - pt, 0), 0)),
            pl.BlockSpec((tm, d), lambda i: (i, 0)),
            pl.BlockSpec((1, ng, 1, d), lambda i: (layer, i, 0, 2)),
            pl.BlockSpec((None, kdim, d), lambda i: (wslot, 0, 0)),
            pl.BlockSpec((1, d), lambda i: (0, 0)),
            pl.BlockSpec((1, d), lambda i: (0, 0)),
        ],
        out_specs=pl.BlockSpec((tm, d), lambda i: (i, 0)),
        compiler_params=_cp(40, ("parallel",)),
        name="proj_ln",
    )(a_prompt, a_sample, x, mods, w, g, b)


def _cm_mix_kernel(u_ref, vraw_ref, x_ref, gate_ref, mx_ref, bias_ref, lvg_ref, lvb_ref,
                   wo_ref, g_ref, b_ref, *rest):
    o_ref, vs_ref, vp_ref, gated_ref = rest[-4:]
    tm, width = u_ref.shape
    gw = width // CM_GROUPS
    v = _ln(vraw_ref[...].astype(F32), lvg_ref[...], lvb_ref[...])
    vs_ref[0] = v
    vp_ref[0] = v
    vb = v.astype(BF16)
    for c in range(tm // CM_CHUNK):
        rows = slice(c * CM_CHUNK, (c + 1) * CM_CHUNK)
        for g in range(CM_GROUPS):
            cols = slice(g * gw, (g + 1) * gw)
            mixed = jnp.dot(mx_ref[0, g], vb[rows, cols], preferred_element_type=F32)
            mixed = mixed + bias_ref[0, :, cols]
            gated_ref[rows, cols] = (u_ref[rows, cols].astype(F32) * mixed).astype(BF16)
    f = jnp.dot(gated_ref[...], wo_ref[...], preferred_element_type=F32)
    o_ref[...] = _residual_ln(x_ref[...], f, gate_ref, g_ref, b_ref)


def _cm_mix(zz, x, mods, layer, mx, bias, lvg, lvb, wo, g, b, n_prompt, slot, n_slots, vs_prev, vp_prev):
    t, d = x.shape
    width = zz.shape[1] // 2
    tm = CM_CHUNK
    ng = tm // GROUP
    n_prompt_tiles = n_prompt // tm
    kind = lambda i: jnp.where(i < n_prompt_tiles, 0, 1)
    extra_specs, extra_args, aliases = [], [], {}
    if vs_prev is not None:
        extra_specs = [pl.BlockSpec(memory_space=pl.ANY), pl.BlockSpec(memory_space=pl.ANY)]
        extra_args = [vs_prev, vp_prev]
        aliases = {11: 1, 12: 2}
    return pl.pallas_call(
        _cm_mix_kernel,
        out_shape=(jax.ShapeDtypeStruct((t, d), F32),
                   jax.ShapeDtypeStruct((n_slots, t - n_prompt, width), F32),
                   jax.ShapeDtypeStruct((n_slots, 2 * tm, width), F32)),
        grid=(t // tm,),
        in_specs=[
            pl.BlockSpec((tm, width), lambda i: (i, 0)),
            pl.BlockSpec((tm, width), lambda i: (i, 1)),
            pl.BlockSpec((tm, d), lambda i: (i, 0)),
            pl.BlockSpec((1, ng, 1, d), lambda i: (layer, i, 0, 2)),
            pl.BlockSpec((1, CM_GROUPS, CM_CHUNK, CM_CHUNK), lambda i: (kind(i), 0, 0, 0)),
            pl.BlockSpec((1, CM_CHUNK, width), lambda i: (kind(i), 0, 0)),
            pl.BlockSpec((1, width), lambda i: (0, 0)),
            pl.BlockSpec((1, width), lambda i: (0, 0)),
            pl.BlockSpec((None, width, d), lambda i: (slot, 0, 0), pipeline_mode=pl.Buffered(1)),
            pl.BlockSpec((1, d), lambda i: (0, 0)),
            pl.BlockSpec((1, d), lambda i: (0, 0)),
        ] + extra_specs,
        out_specs=(pl.BlockSpec((tm, d), lambda i: (i, 0)),
                   pl.BlockSpec((1, tm, width), lambda i: (slot, jnp.maximum(i - n_prompt_tiles, 0), 0)),
                   pl.BlockSpec((1, tm, width), lambda i: (slot, kind(i), 0))),
        scratch_shapes=[pltpu.VMEM((tm, width), BF16)],
        input_output_aliases=aliases,
        compiler_params=_cp(48, ("arbitrary",)),
        name="cm_mix",
    )(zz, zz, x, mods, mx, bias, lvg, lvb, wo, g, b, *extra_args)


def _split3(a):
    hi = a.astype(BF16)
    return hi, (a - hi.astype(F32)).astype(BF16)


def _expert_max(a):
    return jnp.max(jnp.max(a, axis=0, keepdims=True), axis=1, keepdims=True)


def _route_kernel(x_ref, sh_ref, sc_ref, wr_ref, br_ref, xm_ref, cw_ref, mk_ref, cnt_ref):
    i = pl.program_id(0)
    xm = _modulate(x_ref[...], sh_ref, sc_ref)
    _store_token_tiles(xm_ref, _pack_halves(xm))
    xh, xl = _split3(xm)
    wh, wl = _split3(wr_ref[...])
    nt = lambda a, b: lax.dot_general(a, b, (((1,), (1,)), ((), ())), preferred_element_type=F32)
    scores = jax.nn.sigmoid(nt(wh, xh) + (nt(wl, xh) + nt(wh, xl)))
    sel = scores + br_ref[...]
    ne, tm = sel.shape
    ng, per = N_EXPERT_GROUPS, ne // N_EXPERT_GROUPS
    sel3 = sel.reshape(ng, per, tm)
    neg = -jnp.inf
    within = lax.broadcasted_iota(I32, (ng, per, tm), 1)
    m1 = jnp.max(sel3, axis=1, keepdims=True)
    i1 = jnp.min(jnp.where(sel3 == m1, within, per), axis=1, keepdims=True)
    m2 = jnp.max(jnp.where(within == i1, neg, sel3), axis=1, keepdims=True)
    gs = m1 + m2
    gidx = lax.broadcasted_iota(I32, (ng, 1, tm), 0)
    rank = jnp.zeros((ng, 1, tm), I32)
    for g in range(ng):
        row = gs[g:g + 1]
        ahead = jnp.logical_or(row > gs, jnp.logical_and(row == gs, g < gidx))
        rank = rank + ahead.astype(I32)
    allowed = (rank < TOPK_GROUPS).astype(F32)
    ms = jnp.where(allowed > 0.0, sel3, neg)
    eidx = lax.broadcasted_iota(I32, (ng, per, tm), 0) * per + within
    chosen = jnp.zeros((ng, per, tm), F32)
    for _ in range(TOP_K):
        best = _expert_max(ms)
        first = -_expert_max(-jnp.where(ms == best, eidx, ne).astype(F32))
        pick = eidx.astype(F32) == first
        chosen = jnp.where(pick, 1.0, chosen)
        ms = jnp.where(pick, neg, ms)
    chosen = chosen.reshape(ne, tm)
    tw = chosen * scores
    cw_ref[...] = tw / jnp.sum(tw, axis=0, keepdims=True) * ROUTED_SCALE
    mk_ref[...] = chosen.astype(BF16)

    @pl.when(i == 0)
    def _():
        cnt_ref[...] = jnp.zeros_like(cnt_ref)

    cnt_ref[...] += jnp.sum(chosen, axis=1, keepdims=True)


def _route(x, mods, layer, wr_t, br_col):
    t, d = x.shape
    ne = wr_t.shape[0]
    tm = 256
    ng = tm // GROUP
    return pl.pallas_call(
        _route_kernel,
        out_shape=(jax.ShapeDtypeStruct((t * TOKEN_ROWS, LANES), U32), jax.ShapeDtypeStruct((ne, t), F32),
                   jax.ShapeDtypeStruct((ne, t), BF16), jax.ShapeDtypeStruct((ne, 1), F32)),
        grid=(t // tm,),
        in_specs=[
            pl.BlockSpec((tm, d), lambda i: (i, 0)),
            pl.BlockSpec((1, ng, 1, d), lambda i: (layer, i, 0, 0)),
            pl.BlockSpec((1, ng, 1, d), lambda i: (layer, i, 0, 1)),
            pl.BlockSpec((ne, d), lambda i: (0, 0)),
            pl.BlockSpec((ne, 1), lambda i: (0, 0)),
        ],
        out_specs=(pl.BlockSpec((tm * TOKEN_ROWS, LANES), lambda i: (i, 0)),
                   pl.BlockSpec((ne, tm), lambda i: (0, i)),
                   pl.BlockSpec((ne, tm), lambda i: (0, i)), pl.BlockSpec((ne, 1), lambda i: (0, 0))),
        compiler_params=_cp(32, ("arbitrary",)),
        name="router",
    )(x, mods, mods, wr_t, br_col)


def _slots_kernel(mk_ref, cw_ref, start_ref, before_ref, lower_ref, pos_ref, w_ref, carry_ref):
    i = pl.program_id(0)

    @pl.when(i == 0)
    def _():
        carry_ref[...] = jnp.zeros_like(carry_ref)

    mk = mk_ref[...]
    rank = jnp.dot(mk, before_ref[...], preferred_element_type=F32)
    pos = start_ref[...] + carry_ref[...] + rank
    order = jnp.dot(lower_ref[...], mk, preferred_element_type=F32)
    chosen = mk > 0
    cw = cw_ref[...]
    rows_p, rows_w = [], []
    for k in range(TOP_K):
        pick = jnp.logical_and(chosen, order == k)
        rows_p.append(jnp.sum(jnp.where(pick, pos, 0.0), axis=0, keepdims=True))
        rows_w.append(jnp.sum(jnp.where(pick, cw, 0.0), axis=0, keepdims=True))
    pos_ref[...] = jnp.concatenate(rows_p, axis=0).astype(I32)
    w_ref[...] = jnp.concatenate(rows_w, axis=0)
    carry_ref[...] += jnp.sum(mk.astype(F32), axis=1, keepdims=True)


def _slots(mk, cw, start_col):
    ne, t = mk.shape
    tm = 256
    r = jnp.arange(tm)
    before = (r[:, None] < r[None, :]).astype(BF16)
    return pl.pallas_call(
        _slots_kernel,
        out_shape=(jax.ShapeDtypeStruct((TOP_K, t), I32), jax.ShapeDtypeStruct((TOP_K, t), F32)),
        grid=(t // tm,),
        in_specs=[
            pl.BlockSpec((ne, tm), lambda i: (0, i)),
            pl.BlockSpec((ne, tm), lambda i: (0, i)),
            pl.BlockSpec((ne, 1), lambda i: (0, 0)),
            pl.BlockSpec((tm, tm), lambda i: (0, 0)),
            pl.BlockSpec((ne, ne), lambda i: (0, 0)),
        ],
        out_specs=(pl.BlockSpec((TOP_K, tm), lambda i: (0, i)), pl.BlockSpec((TOP_K, tm), lambda i: (0, i))),
        scratch_shapes=[pltpu.VMEM((ne, 1), F32)],
        compiler_params=_cp(32, ("arbitrary",)),
        name="plan",
    )(mk, cw, start_col, before, _strict_lower(ne))


def _dispatch_kernel(fill_ref, pad_ref, pos_ref, xm_ref, wsg_ref, wsu_ref, wsd_ref, xs_ref, sh_ref,
                     zeros_ref, sem):
    i = pl.program_id(0)
    tm = xm_ref.shape[0] // TOKEN_ROWS
    ne = fill_ref.shape[0]
    pad_bits = (zeros_ref.shape[0] // TOKEN_ROWS).bit_length() - 1

    def row_copy(r, k):
        return pltpu.make_async_copy(_token_tile(xm_ref, r), _token_tile(xs_ref, pos_ref[r * TOP_K + k]), sem)

    def fill(e, wait):
        first, n = fill_ref[e], pad_ref[e]
        for bit in range(pad_bits):
            size = (1 << bit) * TOKEN_ROWS
            done = (n >> (bit + 1)) << (bit + 1)
            dst = pl.multiple_of((first + done) * TOKEN_ROWS, TOKEN_ROWS)
            piece = pltpu.make_async_copy(zeros_ref.at[pl.ds(0, size)], xs_ref.at[pl.ds(dst, size)], sem)

            @pl.when(((n >> bit) & 1) == 1)
            def _(piece=piece):
                if wait:
                    piece.wait()
                else:
                    piece.start()

    @pl.when(i == 0)
    def _():
        zeros_ref[...] = jnp.zeros_like(zeros_ref)

        def start(e, c):
            fill(e, False)
            return c

        def wait(e, c):
            fill(e, True)
            return c

        lax.fori_loop(0, ne, start, 0)
        lax.fori_loop(0, ne, wait, 0)

    def start(r, c):
        for k in range(TOP_K):
            row_copy(r, k).start(priority=k % 2)
        return c

    def wait(r, c):
        for k in range(TOP_K):
            row_copy(r, k).wait()
        return c

    lax.fori_loop(0, tm // 2, start, 0)
    lo, hi = _unpack_halves(_load_token_tiles(xm_ref))
    xb = jnp.concatenate([lo, hi], axis=1).astype(BF16)
    hg = jnp.dot(xb, wsg_ref[...], preferred_element_type=F32)
    hu = jnp.dot(xb, wsu_ref[...], preferred_element_type=F32)
    f = jnp.dot((jax.nn.silu(hg) * hu).astype(BF16), wsd_ref[...], preferred_element_type=F32)
    sh_ref[...] = f.astype(sh_ref.dtype)
    lax.fori_loop(tm // 2, tm, start, 0)
    lax.fori_loop(0, tm, wait, 0)


def _dispatch(fill, pad, pos_flat, xm, wsg, wsu, wsd, layer, n_slots):
    t = xm.shape[0] // TOKEN_ROWS
    d, fs = wsg.shape[1], wsg.shape[2]
    tm = 256
    return pl.pallas_call(
        _dispatch_kernel,
        out_shape=(jax.ShapeDtypeStruct((n_slots * TOKEN_ROWS, LANES), U32),
                   jax.ShapeDtypeStruct((t, d), BF16)),
        grid_spec=pltpu.PrefetchScalarGridSpec(
            num_scalar_prefetch=2,
            grid=(t // tm,),
            in_specs=[
                pl.BlockSpec((tm * TOP_K,), lambda i, fill, pad: (i,), memory_space=pltpu.SMEM),
                pl.BlockSpec((tm * TOKEN_ROWS, LANES), lambda i, fill, pad: (i, 0)),
                pl.BlockSpec((None, d, fs), lambda i, fill, pad: (layer, 0, 0)),
                pl.BlockSpec((None, d, fs), lambda i, fill, pad: (layer, 0, 0)),
                pl.BlockSpec((None, fs, d), lambda i, fill, pad: (layer, 0, 0)),
            ],
            out_specs=(pl.BlockSpec(memory_space=pl.ANY),
                       pl.BlockSpec((tm, d), lambda i, fill, pad: (i, 0))),
            scratch_shapes=[pltpu.VMEM((EXPERT_TILE * TOKEN_ROWS, LANES), U32), pltpu.SemaphoreType.DMA(())],
        ),
        compiler_params=_cp(40, ("arbitrary",)),
        name="dispatch",
    )(fill, pad, pos_flat, xm, wsg, wsu, wsd)


def _gmm_kernel(te_ref, nt_ref, nxt_ref, slot_ref, xs_ref, wg_ref, wu_ref, wd_ref, ys_ref,
                fg_ref, fu_ref, fd_ref, wgb_ref, wub_ref, wdb_ref, sems, *, layer):
    i = pl.program_id(0)
    e = te_ref[i]
    s = slot_ref[i]
    prev = te_ref[jnp.maximum(i - 1, 0)]

    def fetch(expert, slot):
        return (pltpu.make_async_copy(wg_ref.at[layer, expert], fg_ref.at[slot], sems.at[slot]),
                pltpu.make_async_copy(wu_ref.at[layer, expert], fu_ref.at[slot], sems.at[slot]),
                pltpu.make_async_copy(wd_ref.at[layer, expert], fd_ref.at[slot], sems.at[slot]))

    @pl.when(i == 0)
    def _():
        for c in fetch(e, s):
            c.start()

    for slot in range(2):
        @pl.when(jnp.logical_and(jnp.logical_or(i == 0, e != prev), s == slot))
        def _(slot=slot):
            for c in fetch(e, slot):
                c.wait()
            wgb_ref[...] = fg_ref[slot].astype(BF16)
            wub_ref[...] = fu_ref[slot].astype(BF16)
            wdb_ref[...] = fd_ref[slot].astype(BF16)

            @pl.when(nxt_ref[i] >= 0)
            def _():
                for c in fetch(nxt_ref[i], 1 - slot):
                    c.start()

    @pl.when(i < nt_ref[0])
    def _():
        lo, hi = _unpack_halves(_load_token_tiles(xs_ref))
        x = jnp.concatenate([lo, hi], axis=1).astype(BF16)
        hg = jnp.dot(x, wgb_ref[...], preferred_element_type=F32)
        hu = jnp.dot(x, wub_ref[...], preferred_element_type=F32)
        hid = (jax.nn.silu(hg) * hu).astype(BF16)
        y = jnp.dot(hid, wdb_ref[...], preferred_element_type=F32)
        _store_token_tiles(ys_ref, _pack_halves(y))


def _gmm(tile_expert, n_tiles_used, tile_next, tile_slot, xs, wg, wu, wd, layer, max_tiles):
    d, f = wg.shape[2], wg.shape[3]
    tm = EXPERT_TILE
    row = lambda i, te, nt, nx, sl: (jnp.minimum(i, nt[0] - 1), 0)
    return pl.pallas_call(
        functools.partial(_gmm_kernel, layer=layer),
        out_shape=jax.ShapeDtypeStruct(xs.shape, U32),
        grid_spec=pltpu.PrefetchScalarGridSpec(
            num_scalar_prefetch=4,
            grid=(max_tiles,),
            in_specs=[
                pl.BlockSpec((tm * TOKEN_ROWS, LANES), row),
                pl.BlockSpec(memory_space=pl.ANY),
                pl.BlockSpec(memory_space=pl.ANY),
                pl.BlockSpec(memory_space=pl.ANY),
            ],
            out_specs=pl.BlockSpec((tm * TOKEN_ROWS, LANES), row),
            scratch_shapes=[pltpu.VMEM((2, d, f), F32), pltpu.VMEM((2, d, f), F32), pltpu.VMEM((2, f, d), F32),
                            pltpu.VMEM((d, f), BF16), pltpu.VMEM((d, f), BF16), pltpu.VMEM((f, d), BF16),
                            pltpu.SemaphoreType.DMA((2,))],
        ),
        compiler_params=_cp(56, ("arbitrary",)),
        name="gmm",
    )(tile_expert, n_tiles_used, tile_next, tile_slot, xs, wg, wu, wd)


def _combine_kernel(pos_ref, nxt_ref, w_ref, sh_ref, x_ref, gate_ref, g_ref, b_ref, ys_ref, o_ref,
                    buf_ref, sems):
    i = pl.program_id(0)
    n = pl.num_programs(0)
    tm, d = x_ref.shape
    half = d // 2
    slot = i % 2

    def copy(p_ref, s, r, k):
        return pltpu.make_async_copy(_token_tile(ys_ref, p_ref[r * TOP_K + k]),
                                     _token_tile(buf_ref.at[s, k], r), sems.at[s])

    def gather(p_ref, s):
        def start(r, c):
            for k in range(TOP_K):
                copy(p_ref, s, r, k).start(priority=k % 2)
            return c
        lax.fori_loop(0, tm, start, 0)

    @pl.when(i == 0)
    def _():
        gather(pos_ref, 0)

    for s in range(2):
        @pl.when(jnp.logical_and(i + 1 < n, slot == 1 - s))
        def _(s=s):
            gather(nxt_ref, s)

    def wait(r, c):
        for k in range(TOP_K):
            copy(pos_ref, slot, r, k).wait()
        return c

    lax.fori_loop(0, tm, wait, 0)
    f = sh_ref[...].astype(F32)
    w = w_ref[...]
    wk = [jnp.broadcast_to(w[:, k:k + 1], (tm, LANES)) for k in range(TOP_K)]
    los, his = [], []
    for c in range(TOKEN_ROWS):
        acc_lo = f[:, c * LANES:(c + 1) * LANES]
        acc_hi = f[:, half + c * LANES:half + (c + 1) * LANES]
        for k in range(TOP_K):
            lo, hi = _unpack_halves(buf_ref[slot, k, pl.ds(c, tm, stride=TOKEN_ROWS), :])
            acc_lo = acc_lo + wk[k] * lo
            acc_hi = acc_hi + wk[k] * hi
        los.append(acc_lo)
        his.append(acc_hi)
    f = jnp.concatenate(los + his, axis=1)
    o_ref[...] = _residual_ln(x_ref[...], f, gate_ref, g_ref, b_ref)


def _combine(pos_flat, w, shared, x, mods, layer, g, b, ys, row0, n_rows):
    d = x.shape[1]
    tm = 256
    ng = tm // GROUP
    off = row0 // tm
    last = n_rows // tm - 1
    return pl.pallas_call(
        _combine_kernel,
        out_shape=jax.ShapeDtypeStruct((n_rows, d), F32),
        grid=(n_rows // tm,),
        in_specs=[
            pl.BlockSpec((tm * TOP_K,), lambda i: (i + off,), memory_space=pltpu.SMEM),
            pl.BlockSpec((tm * TOP_K,), lambda i: (jnp.minimum(i + 1, last) + off,), memory_space=pltpu.SMEM),
            pl.BlockSpec((tm, TOP_K), lambda i: (i + off, 0)),
            pl.BlockSpec((tm, d), lambda i: (i + off, 0)),
            pl.BlockSpec((tm, d), lambda i: (i + off, 0)),
            pl.BlockSpec((1, ng, 1, d), lambda i: (layer, i + off, 0, 2)),
            pl.BlockSpec((1, d), lambda i: (0, 0)),
            pl.BlockSpec((1, d), lambda i: (0, 0)),
            pl.BlockSpec(memory_space=pl.ANY),
        ],
        out_specs=pl.BlockSpec((tm, d), lambda i: (i, 0)),
        scratch_shapes=[pltpu.VMEM((2, TOP_K, tm * TOKEN_ROWS, LANES), U32), pltpu.SemaphoreType.DMA((2,))],
        compiler_params=_cp(48, ("arbitrary",)),
        name="combine",
    )(pos_flat, pos_flat, w, shared, x, mods, g, b, ys)


def _moe(x, mods, layer, wr, br, wg, wu, wd, wsg, wsu, wsd, g, b, splits):
    t = x.shape[0]
    ne = wr.shape[1]
    tile = EXPERT_TILE
    max_tiles = (t * TOP_K) // tile + ne
    xm, cw, mk, cnt = _route(x, mods, layer, wr.T, br.reshape(ne, 1))
    ids = jnp.arange(ne, dtype=I32)
    upto = ids[None, :] <= ids[:, None]
    cnt = cnt[:, 0].astype(I32)
    tiles = (cnt + tile - 1) // tile
    ends = jnp.sum(jnp.where(upto, tiles[None, :], 0), axis=1)
    start = (ends - tiles) * tile
    n_used = ends[-1:]
    tile_ids = jnp.minimum(jnp.arange(max_tiles, dtype=I32), n_used[0] - 1)
    tile_expert = jnp.sum((ends[None, :] <= tile_ids[:, None]).astype(I32), axis=1)
    used = tiles > 0
    later_used = jnp.logical_and(ids[None, :] > ids[:, None], used[None, :])
    next_used = jnp.min(jnp.where(later_used, ids[None, :], ne), axis=1)
    next_used = jnp.where(next_used < ne, next_used, -1).astype(I32)
    slot = ((jnp.sum(jnp.logical_and(upto, used[None, :]).astype(I32), axis=1) - 1) % 2).astype(I32)
    pos, w = _slots(mk, cw, start.astype(F32)[:, None])
    pos_flat = pos.T.reshape(-1)
    w = w.T
    xs, shared = _dispatch(start + cnt, tiles * tile - cnt, pos_flat, xm, wsg, wsu, wsd, layer,
                           max_tiles * tile)
    of_tile = (tile_expert[:, None] == ids[None, :]).astype(I32)
    tile_next = jnp.sum(of_tile * next_used[None, :], axis=1)
    tile_slot = jnp.sum(of_tile * slot[None, :], axis=1)
    ys = _gmm(tile_expert, n_used.astype(I32), tile_next, tile_slot, xs, wg, wu, wd, layer, max_tiles)
    return [_combine(pos_flat, w, shared, x, mods, layer, g, b, ys, row0, n_rows)
            for row0, n_rows in splits]


def _expand_mods(mod, n_prompt_groups, n_streams):
    depth, _, n = mod.shape
    p = jnp.broadcast_to(mod[:, 0:1], (depth, n_prompt_groups, n))
    return jnp.concatenate([p, mod[:, 1:1 + n_streams]], axis=1)[:, :, None, :]


def kernel(x_prompt, x_sample, c_prompt, c_sample, cache_k, cache_v, w_ada_mix, b_ada_mix, w_ada_ffn, b_ada_ffn, ln_mix_g, ln_mix_b, ln_ffn_g, ln_ffn_b, w_qkv, w_o, w_uv, b_uv, ln_v_g, ln_v_b, w_s, b_s, w_cm_out, w_router, b_router, w_gate, w_up, w_down, ws_gate, ws_up, ws_down):
    bp, sp, d = x_prompt.shape
    bs, t_new, _ = x_sample.shape
    assert bp == 1 and t_new == GROUP and sp % ATT_BLOCK == 0 and d == N_HEADS * HEAD_DIM
    n_prompt = bp * sp
    n_sample = bs * t_new
    width = w_uv.shape[2] // 2
    past = cache_k.shape[2]

    x = jnp.concatenate([x_prompt.reshape(n_prompt, d), x_sample.reshape(n_sample, d)], axis=0)

    rows = 8 * ((1 + bs + 7) // 8)
    c_all = jnp.zeros((rows, d), F32).at[0:1].set(c_prompt).at[1:1 + bs].set(c_sample)
    mods_mix = _expand_mods(_ada(c_all, w_ada_mix, b_ada_mix), n_prompt // GROUP, bs)
    mods_ffn = _expand_mods(_ada(c_all, w_ada_ffn, b_ada_ffn), n_prompt // GROUP, bs)

    cpos = jnp.arange(CM_CHUNK)
    cmask = (cpos[None, :] // CHUNK) <= (cpos[:, None] // CHUNK)
    pair = (cpos[None, :] // t_new) == (cpos[:, None] // t_new)
    fold = cpos % t_new
    smask = (fold[None, :] // CHUNK) <= (fold[:, None] // CHUNK)
    ws_prompt = jnp.where(cmask, w_s, 0.0)
    ws_sample = jnp.where(jnp.logical_and(pair, smask), w_s[:, :, fold][:, :, :, fold], 0.0)
    mix = jnp.stack([ws_prompt, ws_sample], axis=1).astype(BF16)
    gw = width // CM_GROUPS
    bias_p = jnp.repeat(jnp.swapaxes(b_s, 1, 2), gw, axis=2)
    mix_bias = jnp.stack([bias_p, bias_p[:, fold]], axis=1)

    ck = cache_k.reshape(cache_k.shape[0], bs, past * N_HEADS, HEAD_DIM)
    cv = cache_v.reshape(cache_v.shape[0], bs, past * N_HEADS, HEAD_DIM)

    n_sb = (DEPTH + 1) // 2
    n_cm = DEPTH // 2
    kp = vp = ksm = vsm = cms = cmp_ = None
    wq, wo = w_qkv.astype(BF16), w_o.astype(BF16)
    wuv, wcm = w_uv.astype(BF16), w_cm_out.astype(BF16)
    wsg, wsu, wsd = ws_gate.astype(BF16), ws_up.astype(BF16), ws_down.astype(BF16)
    buv = b_uv[:, None, :]
    for i in range(DEPTH):
        j = i // 2
        if i % 2 == 0:
            qkv_p, kp, vp = _qkv(x, mods_mix, i, wq, 0, n_prompt, j, n_sb, kp, vp)
            qkv_s, ksm, vsm = _qkv(x, mods_mix, i, wq, n_prompt, n_sample, j, n_sb, ksm, vsm)
            o_p = _attn_prompt(qkv_p, n_prompt)
            o_s, cs = _attn_sample(qkv_s, ck, cv, j, 0, bs, t_new)
            o_s = lax.cond(
                jnp.max(cs) > STICK_DONE,
                lambda: _attn_sample_rest(qkv_s, o_s, cs, ck, cv, j, 0, bs, t_new),
                lambda: o_s)
            x = _proj_ln(o_p, o_s, x, mods_mix, i, wo, j, ln_mix_g[i][None], ln_mix_b[i][None])
        else:
            zz = _modmm(x, mods_mix, i, wuv, buv, j, BF16, True)
            x, cms, cmp_ = _cm_mix(zz, x, mods_mix, i, mix[j], mix_bias[j], ln_v_g[j][None], ln_v_b[j][None],
                                   wcm, ln_mix_g[i][None], ln_mix_b[i][None],
                                   n_prompt, j, n_cm, cms, cmp_)
        last = i == DEPTH - 1
        splits = [(0, n_prompt), (n_prompt, n_sample)] if last else [(0, n_prompt + n_sample)]
        outs = _moe(x, mods_ffn, i, w_router[i], b_router[i][None], w_gate, w_up, w_down,
                    wsg, wsu, wsd, ln_ffn_g[i][None], ln_ffn_b[i][None], splits)
        x = outs[0]

    y_prompt, y_sample = outs
    return (
        y_prompt.reshape(bp, sp, d),
        y_sample.reshape(bs, t_new, d),
        kp.reshape(n_sb, bp, sp, N_HEADS, HEAD_DIM),
        vp.reshape(n_sb, bp, sp, N_HEADS, HEAD_DIM),
        ksm.reshape(n_sb, bs, t_new, N_HEADS, HEAD_DIM),
        vsm.reshape(n_sb, bs, t_new, N_HEADS, HEAD_DIM),
        cmp_[:, :CM_CHUNK].reshape(n_cm, bp, CM_CHUNK, width),
        cms.reshape(n_cm, bs, t_new, width),
    )
```
